```python
import math
import jax, jax.numpy as jnp
from jax import lax
import numpy as np

D_MODEL = 1024
BATCH = 2
SEQ = 8192
DEPTH = 4
DEC_BATCH = 128
DEC_SEQ = 1
PAST_LEN = 8192
PAGE_SIZE = 128

HEAD_DIM = 64
ROPE_THETA = 10000.0
EPS = 1e-6
QBLK = 128
H_A = 8
KV_A = 2
IDX_HEADS = 4
IDX_DIM = 64
TOPK_MAX = 256
H_B = 4
KV_B = 2
H_C = 16
KV_C = 2
WINDOW = 128
D_FF = 2816
CONV_W = 3

N_EVEN = (DEPTH + 1) // 2
N_ODD = DEPTH // 2
EVEN_SPLITS = (H_A * HEAD_DIM, KV_A * HEAD_DIM, KV_A * HEAD_DIM,
               IDX_HEADS * IDX_DIM, IDX_DIM, IDX_HEADS,
               H_B * 2 * HEAD_DIM, KV_B * 2 * HEAD_DIM, KV_B * 2 * HEAD_DIM)
EVEN_SPLIT_AT = tuple(int(c) for c in np.cumsum(EVEN_SPLITS)[:-1])
D_IN_EVEN = int(sum(EVEN_SPLITS))
D_IN_ODD = (H_C + 2 * KV_C) * HEAD_DIM

kernel_name = 'hybrid_dsa_diff_swa_convglu_step'


def rmsnorm(x, g):
    xf = x.astype(jnp.float32)
    y = xf * lax.rsqrt(jnp.mean(xf * xf, axis=-1, keepdims=True) + EPS)
    return (y * g.astype(jnp.float32)).astype(x.dtype)


def rope(x, pos):
    half = x.shape[-1] // 2
    inv = ROPE_THETA ** (-jnp.arange(half, dtype=jnp.float32) / half)
    ang = pos.astype(jnp.float32)[:, None] * inv[None, :]
    cos = jnp.cos(ang)[:, None, :]
    sin = jnp.sin(ang)[:, None, :]
    x1 = x[..., :half].astype(jnp.float32)
    x2 = x[..., half:].astype(jnp.float32)
    return jnp.concatenate([x1 * cos - x2 * sin, x2 * cos + x1 * sin], -1).astype(x.dtype)


def even_project(h, w_in, pos):
    B, T = h.shape[:2]
    qa, ka, va, qi, ki, wi, qb, kb, vb = jnp.split(h @ w_in, EVEN_SPLIT_AT, axis=-1)
    qa = rope(qa.reshape(B, T, H_A, HEAD_DIM), pos)
    ka = rope(ka.reshape(B, T, KV_A, HEAD_DIM), pos)
    va = va.reshape(B, T, KV_A, HEAD_DIM)
    qi = rope(qi.reshape(B, T, IDX_HEADS, IDX_DIM), pos)
    ki = rope(ki.reshape(B, T, 1, IDX_DIM), pos)[:, :, 0]
    wi = wi * IDX_HEADS ** -0.5
    qb = rope(qb.reshape(B, T, H_B * 2, HEAD_DIM), pos).reshape(B, T, H_B, 2 * HEAD_DIM)
    kb = rope(kb.reshape(B, T, KV_B * 2, HEAD_DIM), pos).reshape(B, T, KV_B, 2 * HEAD_DIM)
    vb = vb.reshape(B, T, KV_B, 2 * HEAD_DIM)
    return qa, ka, va, qi, ki, wi, qb, kb, vb


def index_scores(qi, wi, ki):
    s = jnp.einsum('bthd,bld->bthl', qi, ki, preferred_element_type=jnp.float32) * IDX_DIM ** -0.5
    return jnp.einsum('bth,bthl->btl', wi.astype(jnp.float32), jax.nn.relu(s))


def sparse_attend(qa, k_sel, v_sel, valid):
    B, T = qa.shape[:2]
    q = qa.reshape(B, T, KV_A, H_A // KV_A, HEAD_DIM)
    s = jnp.einsum('btkgd,btnkd->btkgn', q, k_sel, preferred_element_type=jnp.float32) * HEAD_DIM ** -0.5
    s = jnp.where(valid[:, :, None, None, :], s, -jnp.inf)
    p = jax.nn.softmax(s, axis=-1)
    o = jnp.einsum('btkgn,btnkd->btkgd', p.astype(v_sel.dtype), v_sel)
    return o.reshape(B, T, H_A * HEAD_DIM)


def prompt_sparse_attn(qa, ka, va, qi, wi, ki):
    B, S = qa.shape[:2]
    n_sel = min(TOPK_MAX, S // 4)
    kpos = jnp.arange(S)
    bidx = jnp.arange(B)[:, None, None]

    def block(i):
        t0 = i * QBLK
        tpos = t0 + jnp.arange(QBLK)
        q_b = lax.dynamic_slice_in_dim(qa, t0, QBLK, axis=1)
        qi_b = lax.dynamic_slice_in_dim(qi, t0, QBLK, axis=1)
        wi_b = lax.dynamic_slice_in_dim(wi, t0, QBLK, axis=1)
        sc = index_scores(qi_b, wi_b, ki)
        sc = jnp.where((kpos[None, :] <= tpos[:, None])[None], sc, -jnp.inf)
        _, idx = lax.top_k(sc, n_sel)
        valid = idx <= tpos[None, :, None]
        return sparse_attend(q_b, ka[bidx, idx], va[bidx, idx], valid)

    o = lax.map(block, jnp.arange(S // QBLK))
    return o.transpose(1, 0, 2, 3).reshape(B, S, -1)


def sample_sparse_attn(qa, ka, va, qi, wi, ki, j, cache_a_kv, cache_a_idx, page_table):
    DB, T = qa.shape[:2]
    P = page_table.shape[1] * PAGE_SIZE
    L = P + T
    n_sel = min(TOPK_MAX, L // 4)
    ki_past = cache_a_idx[j, page_table].reshape(DB, P, IDX_DIM)
    sc = index_scores(qi, wi, jnp.concatenate([ki_past, ki], axis=1))
    tpos = P + jnp.arange(T)
    kpos = jnp.arange(L)
    sc = jnp.where((kpos[None, :] <= tpos[:, None])[None], sc, -jnp.inf)
    _, idx = lax.top_k(sc, n_sel)
    valid = idx <= tpos[None, :, None]
    bidx = jnp.arange(DB)[:, None, None]
    is_past = (idx < P)[..., None, None]
    pidx = jnp.minimum(idx, P - 1)
    phys = page_table[bidx, pidx // PAGE_SIZE]
    kv_past = cache_a_kv[j, phys, pidx % PAGE_SIZE]
    nidx = jnp.clip(idx - P, 0, T - 1)
    k_sel = jnp.where(is_past, kv_past[:, :, :, 0], ka[bidx, nidx])
    v_sel = jnp.where(is_past, kv_past[:, :, :, 1], va[bidx, nidx])
    return sparse_attend(qa, k_sel, v_sel, valid)


def diff_lambda(lp, lam_init):
    lp = lp.astype(jnp.float32)
    return jnp.exp(jnp.sum(lp[0] * lp[1])) - jnp.exp(jnp.sum(lp[2] * lp[3])) + lam_init


def diff_attend(qb, kb, vb, mask, lam, lam_init, subln_g):
    B, T = qb.shape[:2]
    L = kb.shape[1]
    G = H_B // KV_B
    q = qb.reshape(B, T, KV_B, G, 2, HEAD_DIM)
    k = kb.reshape(B, L, KV_B, 2, HEAD_DIM)
    s = jnp.einsum('btkgcd,blkcd->bkgctl', q, k, preferred_element_type=jnp.float32) * HEAD_DIM ** -0.5
    p = jax.nn.softmax(jnp.where(mask, s, -jnp.inf), axis=-1)
    a = p[:, :, :, 0] - lam * p[:, :, :, 1]
    o = jnp.einsum('bkgtl,blke->btkge', a.astype(vb.dtype), vb)
    o = rmsnorm(o, subln_g) * (1.0 - lam_init)
    return o.reshape(B, T, H_B * 2 * HEAD_DIM)


def prompt_diff_attn(qb, kb, vb, lam, lam_init, subln_g):
    B, S = qb.shape[:2]
    kpos = jnp.arange(S)

    def block(i):
        t0 = i * QBLK
        q = lax.dynamic_slice_in_dim(qb, t0, QBLK, axis=1)
        mask = kpos[None, :] <= (t0 + jnp.arange(QBLK))[:, None]
        return diff_attend(q, kb, vb, mask, lam, lam_init, subln_g)

    o = lax.map(block, jnp.arange(S // QBLK))
    return o.transpose(1, 0, 2, 3).reshape(B, S, -1)


def even_prompt(h, w_in, w_out, lam, lam_init, subln_g):
    S = h.shape[1]
    qa, ka, va, qi, ki, wi, qb, kb, vb = even_project(h, w_in, jnp.arange(S))
    oa = prompt_sparse_attn(qa, ka, va, qi, wi, ki)
    ob = prompt_diff_attn(qb, kb, vb, lam, lam_init, subln_g)
    out = jnp.concatenate([oa, ob], axis=-1) @ w_out
    return out, jnp.stack([ka, va], 2), ki, jnp.stack([kb, vb], 2)


def even_sample(h, w_in, w_out, lam, lam_init, subln_g, j, cache_a_kv, cache_a_idx, cache_b_kv, page_table):
    DB, T = h.shape[:2]
    P = page_table.shape[1] * PAGE_SIZE
    pos = P + jnp.arange(T)
    qa, ka, va, qi, ki, wi, qb, kb, vb = even_project(h, w_in, pos)
    oa = sample_sparse_attn(qa, ka, va, qi, wi, ki, j, cache_a_kv, cache_a_idx, page_table)
    kvb_past = cache_b_kv[j, page_table].reshape(DB, P, 2, KV_B, 2 * HEAD_DIM)
    kb_all = jnp.concatenate([kvb_past[:, :, 0], kb], axis=1)
    vb_all = jnp.concatenate([kvb_past[:, :, 1], vb], axis=1)
    mask = jnp.arange(P + T)[None, :] <= pos[:, None]
    ob = diff_attend(qb, kb_all, vb_all, mask, lam, lam_init, subln_g)
    out = jnp.concatenate([oa, ob], axis=-1) @ w_out
    return out, jnp.stack([ka, va], 2), ki, jnp.stack([kb, vb], 2)


def odd_project(h, w_in, pos):
    B, T = h.shape[:2]
    q, k, v = jnp.split(h @ w_in, [H_C * HEAD_DIM, (H_C + KV_C) * HEAD_DIM], axis=-1)
    q = rope(q.reshape(B, T, H_C, HEAD_DIM), pos)
    k = rope(k.reshape(B, T, KV_C, HEAD_DIM), pos)
    return q, k, v.reshape(B, T, KV_C, HEAD_DIM)


def sink_attend(q, k, v, mask, sinks):
    G = H_C // KV_C
    q = q.reshape(q.shape[:-2] + (KV_C, G, HEAD_DIM))
    s = jnp.einsum('...tkgd,...lkd->...kgtl', q, k, preferred_element_type=jnp.float32) * HEAD_DIM ** -0.5
    s = jnp.where(mask, s, -jnp.inf)
    sink = jnp.broadcast_to(sinks.astype(jnp.float32).reshape(KV_C, G, 1, 1), s.shape[:-1] + (1,))
    p = jax.nn.softmax(jnp.concatenate([s, sink], axis=-1), axis=-1)[..., :-1]
    o = jnp.einsum('...kgtl,...lkd->...tkgd', p.astype(v.dtype), v)
    return o.reshape(o.shape[:-3] + (H_C * HEAD_DIM,))


def odd_prompt(h, w_in, w_out, sinks):
    B, S = h.shape[:2]
    q, k, v = odd_project(h, w_in, jnp.arange(S))
    nb = S // WINDOW

    def band(x):
        xb = x.reshape(B, nb, WINDOW, KV_C, HEAD_DIM)
        prev = jnp.pad(xb, ((0, 0), (1, 0), (0, 0), (0, 0), (0, 0)))[:, :-1]
        return jnp.concatenate([prev, xb], axis=2)

    blk = jnp.arange(nb)[:, None, None]
    qpos = blk * WINDOW + jnp.arange(WINDOW)[None, :, None]
    kpos = (blk - 1) * WINDOW + jnp.arange(2 * WINDOW)[None, None, :]
    mask = (kpos >= qpos - WINDOW) & (kpos <= qpos) & (kpos >= 0)
    qb = q.reshape(B, nb, WINDOW, H_C, HEAD_DIM)
    o = sink_attend(qb, band(k), band(v), mask[:, None, None], sinks).reshape(B, S, -1)
    wb = min(WINDOW, S)
    return o @ w_out, jnp.stack([k, v], 2)[:, S - wb:]


def odd_sample(h, w_in, w_out, sinks, buf, P):
    T = h.shape[1]
    pos = P + jnp.arange(T)
    q, k, v = odd_project(h, w_in, pos)
    wb = buf.shape[1]
    kv_all = jnp.concatenate([buf, jnp.stack([k, v], 2)], axis=1)
    kpos = P - wb + jnp.arange(wb + T)
    mask = (kpos[None, :] >= pos[:, None] - WINDOW) & (kpos[None, :] <= pos[:, None])
    o = sink_attend(q, kv_all[:, :, 0], kv_all[:, :, 1], mask, sinks)
    return o @ w_out, kv_all[:, T:]


def conv_glu(h, hist, w_up, conv_w, conv_b, w_down):
    T = h.shape[1]
    g, v = jnp.split(h @ w_up, 2, axis=-1)
    gx = jnp.concatenate([hist, g], axis=1)
    gc = sum(conv_w[c] * gx[:, c:c + T] for c in range(CONV_W)) + conv_b
    y = (jax.nn.gelu(gc, approximate=False) * v) @ w_down
    return y, gx[:, T:]


def setup_inputs(seed: int = 0) -> dict:
    key = jax.random.key(seed)
    ks = jax.random.split(key, 24)
    f32 = jnp.float32
    nrm = lambda k, shape, scale: jax.random.normal(k, shape, f32) * scale
    n_pages = PAST_LEN // PAGE_SIZE
    n_used = DEC_BATCH * n_pages
    n_pool = n_used + max(1, n_used // 4)
    wb = min(WINDOW, PAST_LEN)
    page_table = jax.random.permutation(ks[7], n_pool)[:n_used].reshape(DEC_BATCH, n_pages).astype(jnp.int32)
    return {
        'x_prompt': nrm(ks[0], (BATCH, SEQ, D_MODEL), 1.0),
        'x_sample': nrm(ks[1], (DEC_BATCH, DEC_SEQ, D_MODEL), 1.0),
        'cache_a_kv': nrm(ks[2], (N_EVEN, n_pool, PAGE_SIZE, 2, KV_A, HEAD_DIM), 1.0),
        'cache_a_idx': nrm(ks[3], (N_EVEN, n_pool, PAGE_SIZE, IDX_DIM), 1.0),
        'cache_b_kv': nrm(ks[4], (N_EVEN, n_pool, PAGE_SIZE, 2, KV_B, 2 * HEAD_DIM), 1.0),
        'state_c_kv': nrm(ks[5], (N_ODD, DEC_BATCH, wb, 2, KV_C, HEAD_DIM), 1.0),
        'state_ffn_conv': nrm(ks[6], (DEPTH, DEC_BATCH, CONV_W - 1, D_FF), 1.0),
        'page_table': page_table,
        'norm_mix_g': 1.0 + nrm(ks[8], (DEPTH, D_MODEL), 0.02),
        'norm_ffn_g': 1.0 + nrm(ks[9], (DEPTH, D_MODEL), 0.02),
        'norm_final_g': 1.0 + nrm(ks[10], (D_MODEL,), 0.02),
        'w_in_even': nrm(ks[11], (N_EVEN, D_MODEL, D_IN_EVEN), D_MODEL ** -0.5),
        'w_out_even': nrm(ks[12], (N_EVEN, H_A * HEAD_DIM + H_B * 2 * HEAD_DIM, D_MODEL), D_MODEL ** -0.5),
        'b_lambda': nrm(ks[13], (N_EVEN, 4, HEAD_DIM), 0.1),
        'b_subln_g': 1.0 + nrm(ks[14], (N_EVEN, 2 * HEAD_DIM), 0.02),
        'w_in_odd': nrm(ks[15], (N_ODD, D_MODEL, D_IN_ODD), D_MODEL ** -0.5),
        'w_out_odd': nrm(ks[16], (N_ODD, H_C * HEAD_DIM, D_MODEL), (H_C * HEAD_DIM) ** -0.5),
        'c_sinks': nrm(ks[17], (N_ODD, H_C), 0.5),
        'ffn_w_up': nrm(ks[18], (DEPTH, D_MODEL, 2 * D_FF), D_MODEL ** -0.5),
        'ffn_conv_w': nrm(ks[19], (DEPTH, CONV_W, D_FF), CONV_W ** -0.5),
        'ffn_conv_b': nrm(ks[20], (DEPTH, D_FF), 0.02),
        'ffn_w_down': nrm(ks[21], (DEPTH, D_FF, D_MODEL), D_FF ** -0.5),
    }


def reference(x_prompt, x_sample, cache_a_kv, cache_a_idx, cache_b_kv, state_c_kv, state_ffn_conv, page_table,
              norm_mix_g, norm_ffn_g, norm_final_g, w_in_even, w_out_even, b_lambda, b_subln_g,
              w_in_odd, w_out_odd, c_sinks, ffn_w_up, ffn_conv_w, ffn_conv_b, ffn_w_down):
    P = page_table.shape[1] * PAGE_SIZE
    xp, xs = x_prompt, x_sample
    akv_p, akv_s, aidx_p, aidx_s, bkv_p, bkv_s = [], [], [], [], [], []
    ckv_p, ckv_s, conv_p, conv_s = [], [], [], []
    for l in range(DEPTH):
        hp = rmsnorm(xp, norm_mix_g[l])
        hs = rmsnorm(xs, norm_mix_g[l])
        if l % 2 == 0:
            j = l // 2
            lam_init = 0.8 - 0.6 * math.exp(-0.3 * l)
            lam = diff_lambda(b_lambda[j], lam_init)
            op, a1, a2, a3 = even_prompt(hp, w_in_even[j], w_out_even[j], lam, lam_init, b_subln_g[j])
            os_, s1, s2, s3 = even_sample(hs, w_in_even[j], w_out_even[j], lam, lam_init, b_subln_g[j],
                                          j, cache_a_kv, cache_a_idx, cache_b_kv, page_table)
            akv_p.append(a1); aidx_p.append(a2); bkv_p.append(a3)
            akv_s.append(s1); aidx_s.append(s2); bkv_s.append(s3)
        else:
            j = l // 2
            op, c1 = odd_prompt(hp, w_in_odd[j], w_out_odd[j], c_sinks[j])
            os_, c2 = odd_sample(hs, w_in_odd[j], w_out_odd[j], c_sinks[j], state_c_kv[j], P)
            ckv_p.append(c1); ckv_s.append(c2)
        xp = xp + op
        xs = xs + os_
        hist0 = jnp.zeros((xp.shape[0], CONV_W - 1, D_FF), xp.dtype)
        fp, gp = conv_glu(rmsnorm(xp, norm_ffn_g[l]), hist0, ffn_w_up[l], ffn_conv_w[l], ffn_conv_b[l], ffn_w_down[l])
        fs, gs = conv_glu(rmsnorm(xs, norm_ffn_g[l]), state_ffn_conv[l], ffn_w_up[l], ffn_conv_w[l], ffn_conv_b[l], ffn_w_down[l])
        xp = xp + fp
        xs = xs + fs
        conv_p.append(gp); conv_s.append(gs)
    y_prompt = rmsnorm(xp, norm_final_g)
    y_sample = rmsnorm(xs, norm_final_g)
    return (y_prompt, y_sample,
            jnp.stack(akv_p), jnp.stack(akv_s), jnp.stack(aidx_p), jnp.stack(aidx_s),
            jnp.stack(bkv_p), jnp.stack(bkv_s), jnp.stack(ckv_p), jnp.stack(ckv_s),
            jnp.stack(conv_p), jnp.stack(conv_s))
```

```python
import functools
import math

import jax
import jax.numpy as jnp
from jax import lax
from jax.experimental import pallas as pl
from jax.experimental.pallas import tpu as pltpu

F32, BF16, I32 = jnp.float32, jnp.bfloat16, jnp.int32

HEAD_DIM = 64
ROPE_THETA = 10000.0
EPS = 1e-6
H_A, KV_A, IDX_HEADS, IDX_DIM, TOPK_MAX = 8, 2, 4, 64, 256
H_B, KV_B = 4, 2
H_C, KV_C, WINDOW = 16, 2, 128
CONV_W = 3
PAGE_SIZE = 128
DEPTH = 4

LANES = 128
SUBLANES = 8
VMEM_LIMIT = 56 * 1024 * 1024

TQ = 128
CH = 4
NEG = -1e30
PAD_KEY = -2139095041
SCALE = HEAD_DIM ** -0.5

_NT = (((1,), (1,)), ((), ()))


def _cp(*sem):
    return pltpu.CompilerParams(dimension_semantics=sem, vmem_limit_bytes=VMEM_LIMIT)


def _pick_tile(n, target):
    if n <= target:
        return n
    best = None
    for t in range(LANES, target + 1, LANES):
        if n % t == 0:
            best = t
    assert best is not None, (n, target)
    return best


def _rmsnorm_rows(x, g):
    return x * lax.rsqrt(jnp.mean(x * x, axis=-1, keepdims=True) + EPS) * g


def _to_key(x):
    b = lax.bitcast_convert_type(x, I32)
    return jnp.where(b < 0, b ^ jnp.int32(0x7FFFFFFF), b)


def _rope_tables(pos):
    half = HEAD_DIM // 2
    inv = ROPE_THETA ** (-jnp.arange(half, dtype=F32) / half)
    ang = pos.astype(F32)[:, None] * inv[None, :]
    c, s = jnp.cos(ang), jnp.sin(ang)
    return jnp.tile(c, (1, 4)), jnp.concatenate([-s, s, -s, s], axis=1)


def _proj_kernel(x_ref, g_ref, w_ref, cos_ref, sin_ref, *out_refs, plan):
    h = _rmsnorm_rows(x_ref[...], g_ref[...]).astype(BF16)
    cos, sin = cos_ref[...], sin_ref[...]
    lane = lax.broadcasted_iota(I32, cos.shape, 1)
    first_half = (lane % HEAD_DIM) < HEAD_DIM // 2
    for o_idx, col0, width, rope_w, store_w in plan:
        y = jnp.dot(h, w_ref[:, col0:col0 + width], preferred_element_type=F32)
        for c in range(width // LANES):
            yc = y[:, c * LANES:(c + 1) * LANES]
            if c * LANES < rope_w:
                partner = jnp.where(first_half, pltpu.roll(yc, LANES - HEAD_DIM // 2, 1),
                                    pltpu.roll(yc, HEAD_DIM // 2, 1))
                yc = yc * cos + partner * sin
            ow = min(LANES, store_w - c * LANES)
            out_refs[o_idx][:, c * LANES:c * LANES + ow] = yc[:, :ow]


def _proj(x, g, w, cos, sin, plan, out_widths, tm):
    m, d = x.shape
    n_pos = cos.shape[0] // tm
    return pl.pallas_call(
        functools.partial(_proj_kernel, plan=plan),
        grid=(m // tm,),
        in_specs=[pl.BlockSpec((tm, d), lambda i: (i, 0)),
                  pl.BlockSpec((1, d), lambda i: (0, 0)),
                  pl.BlockSpec(w.shape, lambda i: (0, 0)),
                  pl.BlockSpec((tm, LANES), lambda i: (i % n_pos, 0)),
                  pl.BlockSpec((tm, LANES), lambda i: (i % n_pos, 0))],
        out_specs=[pl.BlockSpec((tm, ow), lambda i: (i, 0)) for ow in out_widths],
        out_shape=[jax.ShapeDtypeStruct((m, ow), F32) for ow in out_widths],
        compiler_params=_cp("arbitrary"), name="norm_proj_rope",
    )(x, g.reshape(1, d), w, cos, sin)


def _even_weights(w):
    sizes = (H_A * HEAD_DIM, KV_A * HEAD_DIM, KV_A * HEAD_DIM, IDX_HEADS * IDX_DIM, IDX_DIM, IDX_HEADS,
             H_B * 2 * HEAD_DIM, KV_B * 2 * HEAD_DIM, KV_B * 2 * HEAD_DIM)
    offs = [0]
    for s in sizes:
        offs.append(offs[-1] + s)
    qa, ka, va, qi, ki, wi, qb, kb, vb = (w[:, offs[i]:offs[i + 1]] for i in range(9))
    ki = jnp.pad(ki, ((0, 0), (0, LANES - IDX_DIM)))
    wi = jnp.repeat(wi * (IDX_HEADS ** -0.5 * IDX_DIM ** -0.5), LANES, axis=1)
    return jnp.concatenate([qa, ka, va, qi, ki, wi, qb, kb, vb], axis=1).astype(BF16)


_EVEN_WIDTHS = (512, 256, 256, 64, 512, 512, 512)
_EVEN_PLAN = ((0, 0, 512, 512, 512), (1, 512, 256, 128, 256), (2, 768, 256, 256, 256), (3, 1024, 128, 128, 64),
              (4, 1152, 512, 0, 512), (5, 1664, 512, 512, 512), (6, 2176, 512, 256, 512))
_ODD_WIDTHS = (1024, 256)
_ODD_PLAN = ((0, 0, 1024, 1024, 1024), (1, 1024, 256, 128, 256))


def _outproj_kernel(*refs, n_in):
    o_refs, x_ref, w_refs, y_ref = refs[:n_in], refs[n_in], refs[n_in + 1:2 * n_in + 1], refs[2 * n_in + 1]
    acc = x_ref[...]
    for o_ref, w_ref in zip(o_refs, w_refs):
        acc = acc + jnp.dot(o_ref[...].astype(BF16), w_ref[...], preferred_element_type=F32)
    y_ref[...] = acc


def _outproj(os_, x, ws, tm):
    m, d = x.shape
    n = len(os_)
    return pl.pallas_call(
        functools.partial(_outproj_kernel, n_in=n),
        grid=(m // tm,),
        in_specs=([pl.BlockSpec((tm, o.shape[1]), lambda i: (i, 0)) for o in os_]
                  + [pl.BlockSpec((tm, d), lambda i: (i, 0))]
                  + [pl.BlockSpec(w.shape, lambda i: (0, 0)) for w in ws]),
        out_specs=pl.BlockSpec((tm, d), lambda i: (i, 0)),
        out_shape=jax.ShapeDtypeStruct((m, d), F32),
        compiler_params=_cp("arbitrary"), name="out_proj_residual",
    )(*os_, x, *ws)


def _final_norm_kernel(x_ref, g_ref, y_ref):
    y_ref[...] = _rmsnorm_rows(x_ref[...], g_ref[...])


def _final_norm(x, g, tm):
    m, d = x.shape
    return pl.pallas_call(
        _final_norm_kernel, grid=(m // tm,),
        in_specs=[pl.BlockSpec((tm, d), lambda i: (i, 0)), pl.BlockSpec((1, d), lambda i: (0, 0))],
        out_specs=pl.BlockSpec((tm, d), lambda i: (i, 0)),
        out_shape=jax.ShapeDtypeStruct((m, d), F32),
        compiler_params=_cp("arbitrary"), name="final_norm",
    )(x, g.reshape(1, d))


def _ffn_kernel(*refs, seq_mode, tm, rows_per_seq, nf):
    if seq_mode:
        x_ref, gn_ref, wg_ref, wv_ref, cw_ref, cb_ref, wd_ref, y_ref, tail_ref, h_s, acc_s, gs_s, carry_s = refs
    else:
        (x_ref, gn_ref, wg_ref, wv_ref, cw_ref, cb_ref, wd_ref, gm2_ref, gm1_ref,
         y_ref, gout_ref, h_s, acc_s) = refs
    m, f = pl.program_id(0), pl.program_id(1)

    @pl.when(f == 0)
    def _():
        h_s[...] = _rmsnorm_rows(x_ref[...], gn_ref[...]).astype(BF16)
        acc_s[...] = jnp.zeros_like(acc_s)

    h = h_s[...]
    g = jnp.dot(h, wg_ref[...], preferred_element_type=F32)
    v = jnp.dot(h, wv_ref[...], preferred_element_type=F32)
    if seq_mode:
        starts_seq = (m * tm) % rows_per_seq == 0

        @pl.when(starts_seq)
        def _():
            gs_s[0:SUBLANES, :] = jnp.zeros((SUBLANES, gs_s.shape[1]), F32)

        @pl.when(jnp.logical_not(starts_seq))
        def _():
            gs_s[0:SUBLANES, :] = carry_s[f]

        gs_s[SUBLANES:SUBLANES + tm, :] = g
        last = g[tm - SUBLANES:tm, :]
        carry_s[f] = last
        tail_ref[0] = last
        gm1 = gs_s[SUBLANES - 1:SUBLANES - 1 + tm, :]
        gm2 = gs_s[SUBLANES - 2:SUBLANES - 2 + tm, :]
    else:
        gm1, gm2 = gm1_ref[...], gm2_ref[...]
        gout_ref[...] = g
    cw = cw_ref[...]
    gc = cw[0:1, :] * gm2 + cw[1:2, :] * gm1 + cw[2:3, :] * g + cb_ref[...]
    act = 0.5 * gc * (1.0 + lax.erf(gc * math.sqrt(0.5))) * v
    acc_s[...] += jnp.dot(act.astype(BF16), wd_ref[...], preferred_element_type=F32)

    @pl.when(f == nf - 1)
    def _():
        y_ref[...] = x_ref[...] + acc_s[...]


def _ffn(x, gn, wg, wv, cw, cb, wd, tm, tf, rows_per_seq=None, hist=None):
    m, d = x.shape
    ff = wg.shape[1]
    nf = ff // tf
    seq_mode = hist is None
    common = [pl.BlockSpec((tm, d), lambda i, f: (i, 0)),
              pl.BlockSpec((1, d), lambda i, f: (0, 0)),
              pl.BlockSpec((d, tf), lambda i, f: (0, f)),
              pl.BlockSpec((d, tf), lambda i, f: (0, f)),
              pl.BlockSpec((CONV_W, tf), lambda i, f: (0, f)),
              pl.BlockSpec((1, tf), lambda i, f: (0, f)),
              pl.BlockSpec((tf, d), lambda i, f: (f, 0))]
    y_spec = pl.BlockSpec((tm, d), lambda i, f: (i, 0))
    y_shape = jax.ShapeDtypeStruct((m, d), F32)
    scratch = [pltpu.VMEM((tm, d), BF16), pltpu.VMEM((tm, d), F32)]
    args = [x, gn.reshape(1, d), wg, wv, cw, cb.reshape(1, ff), wd]
    if seq_mode:
        assert rows_per_seq % tm == 0 and tm >= SUBLANES
        in_specs = common
        out_specs = [y_spec, pl.BlockSpec((1, SUBLANES, tf), lambda i, f: (i, 0, f))]
        out_shape = [y_shape, jax.ShapeDtypeStruct((m // tm, SUBLANES, ff), F32)]
        scratch += [pltpu.VMEM((tm + SUBLANES, tf), F32), pltpu.VMEM((nf, SUBLANES, tf), F32)]
    else:
        in_specs = common + [pl.BlockSpec((tm, tf), lambda i, f: (i, f)), pl.BlockSpec((tm, tf), lambda i, f: (i, f))]
        out_specs = [y_spec, pl.BlockSpec((tm, tf), lambda i, f: (i, f))]
        out_shape = [y_shape, jax.ShapeDtypeStruct((m, ff), F32)]
        args += [hist[:, 0, :], hist[:, 1, :]]
    return pl.pallas_call(
        functools.partial(_ffn_kernel, seq_mode=seq_mode, tm=tm, rows_per_seq=rows_per_seq, nf=nf),
        grid=(m // tm, nf), in_specs=in_specs, out_specs=out_specs, out_shape=out_shape,
        scratch_shapes=scratch, compiler_params=_cp("arbitrary", "arbitrary"),
        name="conv_glu_seq" if seq_mode else "conv_glu_tok",
    )(*args)


def _count(key_s, nchunks, pred):
    rows = key_s.shape[1]

    def body(c, acc):
        for u in range(CH):
            t = c * CH + u
            acc = acc + jnp.where(pred(key_s[t], t), 1.0, 0.0)
        return acc

    acc = lax.fori_loop(0, nchunks, body, jnp.zeros((rows, LANES), F32))
    return jnp.sum(acc, axis=1, keepdims=True)


def _select_threshold(key_s, nchunks, kk, idx_bits):
    rows = key_s.shape[1]
    shape = (rows, LANES)
    lane = lax.broadcasted_iota(I32, shape, 1)
    c0 = _count(key_s, nchunks, lambda k, t: k >= 0)
    tau = jnp.where(c0 >= kk, jnp.int32(0), jnp.int32(-2 ** 31))

    def value_bit(i, tau):
        cand = tau + lax.shift_left(jnp.int32(1), (30 - i).astype(I32))
        cand_b = jnp.broadcast_to(cand, shape)
        c = _count(key_s, nchunks, lambda k, t: k >= cand_b)
        return jnp.where(c >= kk, cand, tau)

    tau = lax.fori_loop(0, 31, value_bit, tau)
    tau_b = jnp.broadcast_to(tau, shape)
    need = kk - _count(key_s, nchunks, lambda k, t: k > tau_b)

    def index_bit(i, cut):
        cand = cut + lax.shift_left(jnp.int32(1), (idx_bits - 1 - i).astype(I32))
        cand_b = jnp.broadcast_to(cand, shape)
        c = _count(key_s, nchunks, lambda k, t: (k == tau_b) & (lane < cand_b - t * LANES))
        return jnp.where(c < need, cand, cut)

    cut = lax.fori_loop(0, idx_bits, index_bit, jnp.zeros((rows, 1), I32))
    return tau, cut


def _softmax_update(s, v, m_ref, l_ref, acc_ref, k, v_is_transposed=False):
    m_old = m_ref[k]
    m_new = jnp.maximum(m_old, jnp.max(s, axis=1, keepdims=True))
    alpha = jnp.exp(m_old - m_new)
    p = jnp.exp(s - m_new)
    l_ref[k] = alpha * l_ref[k] + jnp.sum(p, axis=1, keepdims=True)
    if v_is_transposed:
        pv = lax.dot_general(p.astype(BF16), v, _NT, preferred_element_type=F32)
    else:
        pv = jnp.dot(p.astype(BF16), v, preferred_element_type=F32)
    acc_ref[k] = alpha * acc_ref[k] + pv
    m_ref[k] = m_new


def _pad_head(chunk, head, slab):
    lane = lax.broadcasted_iota(I32, chunk.shape, 1)
    if head % 2 != slab:
        chunk = pltpu.roll(chunk, HEAD_DIM, 1)
    keep = (lane < HEAD_DIM) if slab == 0 else (lane >= HEAD_DIM)
    return jnp.where(keep, chunk, 0.0)


def _gather_heads(o_ref, pieces, slabs):
    lane = lax.broadcasted_iota(I32, pieces[0].shape, 1)
    lo = lane < HEAD_DIM
    for j in range(len(pieces) // 2):
        a, b = pieces[2 * j], pieces[2 * j + 1]
        if slabs[2 * j] != 0:
            a = pltpu.roll(a, HEAD_DIM, 1)
        if slabs[2 * j + 1] != 1:
            b = pltpu.roll(b, HEAD_DIM, 1)
        o_ref[:, j * LANES:(j + 1) * LANES] = jnp.where(lo, a, b)


def _init_softmax(m_ref, l_ref, acc_ref):
    m_ref[...] = jnp.full(m_ref.shape, NEG, F32)
    l_ref[...] = jnp.zeros_like(l_ref)
    acc_ref[...] = jnp.zeros_like(acc_ref)


def _attn_a_kernel(qa_ref, qi_ref, wi_ref, ki_ref, kva_ref, o_ref, key_s, q_s, m_s, l_s, acc_s, *, n_sel, idx_bits):
    i = pl.program_id(1)
    nk = i + 1
    nch = (nk + CH - 1) // CH
    grp = H_A // KV_A
    row = lax.broadcasted_iota(I32, (TQ, LANES), 0)
    lane = lax.broadcasted_iota(I32, (TQ, LANES), 1)
    tpos = i * TQ + row

    qi = qi_ref[...].astype(BF16)
    qs = jnp.concatenate([qi[:, h * IDX_DIM:(h + 1) * IDX_DIM] for h in range(IDX_HEADS)], axis=0)
    ws = jnp.concatenate([wi_ref[:, h * LANES:(h + 1) * LANES] for h in range(IDX_HEADS)], axis=0)

    def score_tile(kt, carry):
        kt_rows = ki_ref[pl.ds(pl.multiple_of(kt * TQ, TQ), TQ), :]
        s = lax.dot_general(qs, kt_rows, _NT, preferred_element_type=F32)
        x = jnp.maximum(s, 0.0) * ws
        sc = x[0:TQ]
        for h in range(1, IDX_HEADS):
            sc = sc + x[h * TQ:(h + 1) * TQ]
        sc = jnp.where(kt * TQ + lane <= tpos, sc, -jnp.inf)
        key_s[kt] = _to_key(sc)
        return carry

    lax.fori_loop(0, nk, score_tile, 0)

    def pad_tile(kt, carry):
        key_s[kt] = jnp.full((TQ, LANES), PAD_KEY, I32)
        return carry

    lax.fori_loop(nk, nch * CH, pad_tile, 0)

    row1 = lax.broadcasted_iota(I32, (TQ, 1), 0)
    kk = jnp.minimum(n_sel, i * TQ + row1 + 1).astype(F32)
    tau, cut = _select_threshold(key_s, nch, kk, idx_bits)
    tau_b = jnp.broadcast_to(tau, (TQ, LANES))
    cut_b = jnp.broadcast_to(cut, (TQ, LANES))

    qa = qa_ref[...]
    for k in range(KV_A):
        rows = []
        for g in range(grp):
            h = k * grp + g
            rows.append(_pad_head(qa[:, (h // 2) * LANES:(h // 2 + 1) * LANES], h, k))
        q_s[k] = jnp.concatenate(rows, axis=0).astype(BF16)
    _init_softmax(m_s, l_s, acc_s)

    def attend_tile(kt, carry):
        kv = kva_ref[pl.ds(pl.multiple_of(kt * TQ, TQ), TQ), :]
        k_slab, v_slab = kv[:, 0:LANES], kv[:, LANES:2 * LANES]
        key = key_s[kt]
        sel = (key > tau_b) | ((key == tau_b) & (kt * TQ + lane <= cut_b))
        bias = jnp.where(sel, 0.0, NEG)
        bias = jnp.concatenate([bias] * grp, axis=0)
        for k in range(KV_A):
            s = lax.dot_general(q_s[k], k_slab, _NT, preferred_element_type=F32) * SCALE + bias
            _softmax_update(s, v_slab, m_s, l_s, acc_s, k)
        return carry

    lax.fori_loop(0, nk, attend_tile, 0)

    pieces, slabs = [], []
    for k in range(KV_A):
        o = acc_s[k] / l_s[k]
        for g in range(grp):
            pieces.append(o[g * TQ:(g + 1) * TQ])
            slabs.append(k)
    _gather_heads(o_ref, pieces, slabs)


def _attn_a(qa, qi, wi, ki_bf, kva_bf, nb, s):
    nq = s // TQ
    n_sel = min(TOPK_MAX, s // 4)
    grp = H_A // KV_A
    return pl.pallas_call(
        functools.partial(_attn_a_kernel, n_sel=n_sel, idx_bits=(s - 1).bit_length()),
        grid=(nb, nq),
        in_specs=[pl.BlockSpec((TQ, qa.shape[1]), lambda b, i: (b * nq + i, 0)),
                  pl.BlockSpec((TQ, qi.shape[1]), lambda b, i: (b * nq + i, 0)),
                  pl.BlockSpec((TQ, wi.shape[1]), lambda b, i: (b * nq + i, 0)),
                  pl.BlockSpec((s, IDX_DIM), lambda b, i: (b, 0)),
                  pl.BlockSpec((s, 2 * LANES), lambda b, i: (b, 0))],
        out_specs=pl.BlockSpec((TQ, H_A * HEAD_DIM), lambda b, i: (b * nq + i, 0)),
        out_shape=jax.ShapeDtypeStruct((nb * s, H_A * HEAD_DIM), F32),
        scratch_shapes=[pltpu.VMEM((-(-nq // CH) * CH, TQ, LANES), I32),
                        pltpu.VMEM((KV_A, grp * TQ, LANES), BF16),
                        pltpu.VMEM((KV_A, grp * TQ, 1), F32),
                        pltpu.VMEM((KV_A, grp * TQ, 1), F32),
                        pltpu.VMEM((KV_A, grp * TQ, LANES), F32)],
        compiler_params=_cp("arbitrary", "arbitrary"), name="attn_a_prompt",
    )(qa, qi, wi, ki_bf, kva_bf)


def _lambda_value(lam_ref, lam_init):
    lp = lam_ref[...]
    a = jnp.sum(lp[0:1, :] * lp[1:2, :], axis=1, keepdims=True)
    b = jnp.sum(lp[2:3, :] * lp[3:4, :], axis=1, keepdims=True)
    return jnp.exp(a) - jnp.exp(b) + lam_init


def _attn_b_kernel(qb_ref, kvb_ref, lam_ref, sg_ref, o_ref, q_s, m_s, l_s, acc_s, *, lam_init):
    i = pl.program_id(1)
    grp = H_B // KV_B
    row = lax.broadcasted_iota(I32, (TQ, LANES), 0)
    lane = lax.broadcasted_iota(I32, (TQ, LANES), 1)
    tpos = i * TQ + row
    lo = lane < HEAD_DIM
    qb = qb_ref[...]
    for k in range(KV_B):
        rows = []
        for g in range(grp):
            chunk = qb[:, (k * grp + g) * LANES:(k * grp + g + 1) * LANES]
            rows.append(jnp.where(lo, chunk, 0.0))
            rows.append(jnp.where(lo, 0.0, chunk))
        q_s[k] = jnp.concatenate(rows, axis=0).astype(BF16)
    _init_softmax(m_s, l_s, acc_s)

    def attend_tile(kt, carry):
        kv = kvb_ref[pl.ds(pl.multiple_of(kt * TQ, TQ), TQ), :]
        bias = jnp.where(kt * TQ + lane <= tpos, 0.0, NEG)
        bias = jnp.concatenate([bias] * (2 * grp), axis=0)
        for k in range(KV_B):
            k_slab = kv[:, k * LANES:(k + 1) * LANES]
            v_slab = kv[:, (KV_B + k) * LANES:(KV_B + k + 1) * LANES]
            s = lax.dot_general(q_s[k], k_slab, _NT, preferred_element_type=F32) * SCALE + bias
            _softmax_update(s, v_slab, m_s, l_s, acc_s, k)
        return carry

    lax.fori_loop(0, i + 1, attend_tile, 0)

    lam = _lambda_value(lam_ref, lam_init)
    for k in range(KV_B):
        o = acc_s[k] / l_s[k]
        for g in range(grp):
            d = o[(2 * g) * TQ:(2 * g + 1) * TQ] - lam * o[(2 * g + 1) * TQ:(2 * g + 2) * TQ]
            h = k * grp + g
            o_ref[:, h * LANES:(h + 1) * LANES] = _rmsnorm_rows(d, sg_ref[...]) * (1.0 - lam_init)


def _attn_b(qb, kvb_bf, lam_p, subln_g, lam_init, nb, s):
    nq = s // TQ
    grp = H_B // KV_B
    rows = 2 * grp * TQ
    return pl.pallas_call(
        functools.partial(_attn_b_kernel, lam_init=lam_init),
        grid=(nb, nq),
        in_specs=[pl.BlockSpec((TQ, qb.shape[1]), lambda b, i: (b * nq + i, 0)),
                  pl.BlockSpec((s, kvb_bf.shape[1]), lambda b, i: (b, 0)),
                  pl.BlockSpec(lam_p.shape, lambda b, i: (0, 0)),
                  pl.BlockSpec((1, 2 * HEAD_DIM), lambda b, i: (0, 0))],
        out_specs=pl.BlockSpec((TQ, H_B * 2 * HEAD_DIM), lambda b, i: (b * nq + i, 0)),
        out_shape=jax.ShapeDtypeStruct((nb * s, H_B * 2 * HEAD_DIM), F32),
        scratch_shapes=[pltpu.VMEM((KV_B, rows, LANES), BF16),
                        pltpu.VMEM((KV_B, rows, 1), F32),
                        pltpu.VMEM((KV_B, rows, 1), F32),
                        pltpu.VMEM((KV_B, rows, LANES), F32)],
        compiler_params=_cp("arbitrary", "arbitrary"), name="attn_b_prompt",
    )(qb, kvb_bf, lam_p, subln_g.reshape(1, 2 * HEAD_DIM))


def _attn_c_kernel(q_ref, kv_ref, sink_ref, o_ref):
    i = pl.program_id(1)
    grp = H_C // KV_C
    base = jnp.maximum(i - 1, 0) * TQ
    kv = kv_ref[pl.ds(pl.multiple_of(base, TQ), 2 * TQ), :]
    k_slab, v_slab = kv[:, 0:LANES], kv[:, LANES:2 * LANES]
    qpos = i * TQ + lax.broadcasted_iota(I32, (TQ, 2 * TQ), 0)
    kpos = base + lax.broadcasted_iota(I32, (TQ, 2 * TQ), 1)
    bias = jnp.where((kpos >= qpos - WINDOW) & (kpos <= qpos), 0.0, NEG)
    bias = jnp.concatenate([bias] * grp, axis=0)
    q = q_ref[...]
    pieces, slabs = [], []
    for k in range(KV_C):
        rows, sinks = [], []
        for g in range(grp):
            h = k * grp + g
            rows.append(_pad_head(q[:, (h // 2) * LANES:(h // 2 + 1) * LANES], h, k))
            sinks.append(jnp.broadcast_to(sink_ref[h:h + 1, :], (TQ, 1)))
        qk = jnp.concatenate(rows, axis=0).astype(BF16)
        sink = jnp.concatenate(sinks, axis=0)
        s = lax.dot_general(qk, k_slab, _NT, preferred_element_type=F32) * SCALE + bias
        m = jnp.maximum(jnp.max(s, axis=1, keepdims=True), sink)
        p = jnp.exp(s - m)
        den = jnp.sum(p, axis=1, keepdims=True) + jnp.exp(sink - m)
        o = jnp.dot(p.astype(BF16), v_slab, preferred_element_type=F32) / den
        for g in range(grp):
            pieces.append(o[g * TQ:(g + 1) * TQ])
            slabs.append(k)
    _gather_heads(o_ref, pieces, slabs)


def _attn_c(q, kv_bf, sinks, nb, s):
    nq = s // TQ
    assert nq >= 2 and TQ == WINDOW
    return pl.pallas_call(
        _attn_c_kernel, grid=(nb, nq),
        in_specs=[pl.BlockSpec((TQ, q.shape[1]), lambda b, i: (b * nq + i, 0)),
                  pl.BlockSpec((s, 2 * LANES), lambda b, i: (b, 0)),
                  pl.BlockSpec((H_C, 1), lambda b, i: (0, 0))],
        out_specs=pl.BlockSpec((TQ, H_C * HEAD_DIM), lambda b, i: (b * nq + i, 0)),
        out_shape=jax.ShapeDtypeStruct((nb * s, H_C * HEAD_DIM), F32),
        compiler_params=_cp("arbitrary", "arbitrary"), name="attn_c_prompt",
    )(q, kv_bf, sinks.reshape(H_C, 1))


def _page_specs(rows, layer, pps):
    def make(u):
        return pl.BlockSpec((None, None, rows, LANES), lambda b, g, pt, *_: (layer, pt[b, g * pps + u], 0, 0))
    return [make(u) for u in range(pps)]


def _sa1_kernel(pt_ref, q_ref, w_ref, *refs, pps):
    pages, o_ref = refs[:pps], refs[pps]
    q, w = q_ref[...], w_ref[...]
    for u in range(pps):
        s = jnp.dot(q, pages[u][...].astype(BF16), preferred_element_type=F32)
        o_ref[:, u * LANES:(u + 1) * LANES] = jnp.sum(jnp.maximum(s, 0.0) * w, axis=0, keepdims=True)


def _sa1(page_table, q_blk, w_blk, idx_t, layer, pps):
    nseq, npages = page_table.shape
    gs = pltpu.PrefetchScalarGridSpec(
        num_scalar_prefetch=1, grid=(nseq, npages // pps),
        in_specs=[pl.BlockSpec((None, SUBLANES, IDX_DIM), lambda b, g, pt: (b, 0, 0)),
                  pl.BlockSpec((None, SUBLANES, LANES), lambda b, g, pt: (b, 0, 0))] + _page_specs(IDX_DIM, layer, pps),
        out_specs=pl.BlockSpec((None, 1, pps * LANES), lambda b, g, pt: (b, 0, g)))
    return pl.pallas_call(
        functools.partial(_sa1_kernel, pps=pps), grid_spec=gs,
        out_shape=jax.ShapeDtypeStruct((nseq, 1, npages * PAGE_SIZE), F32),
        compiler_params=_cp("arbitrary", "arbitrary"), name="index_scores_sample",
    )(page_table, q_blk, w_blk, *([idx_t] * pps))


def _sa2_kernel(i_ref, qi_ref, ki_ref, wi_ref, tau_ref, cut_ref, new_ref, key_s, *, n_sel, idx_bits):
    nseq, p = i_ref.shape
    nt = p // LANES
    for t in range(nt):
        key_s[t] = _to_key(i_ref[:, t * LANES:(t + 1) * LANES])
    qi = qi_ref[...].astype(BF16).astype(F32)
    ki = ki_ref[...].astype(BF16).astype(F32)
    inew = jnp.zeros((nseq, 1), F32)
    for h in range(IDX_HEADS):
        sh = jnp.sum(qi[:, h * IDX_DIM:(h + 1) * IDX_DIM] * ki, axis=1, keepdims=True)
        inew = inew + jnp.maximum(sh, 0.0) * wi_ref[:, h * LANES:h * LANES + 1]
    lane = lax.broadcasted_iota(I32, (nseq, LANES), 1)
    key_new = _to_key(jnp.broadcast_to(inew, (nseq, LANES)))
    key_s[nt] = jnp.where(lane == 0, key_new, PAD_KEY)
    for t in range(nt + 1, key_s.shape[0]):
        key_s[t] = jnp.full((nseq, LANES), PAD_KEY, I32)
    kk = jnp.full((nseq, 1), n_sel, F32)
    tau, cut = _select_threshold(key_s, key_s.shape[0] // CH, kk, idx_bits)
    tau_ref[...] = tau
    cut_ref[...] = cut
    kn = key_new[:, 0:1]
    new_ref[...] = ((kn > tau) | ((kn == tau) & (p <= cut))).astype(I32)


def _sa2(scores, qi, ki, wi):
    nseq, p = scores.shape
    ntiles = -(-(p // LANES + 1) // CH) * CH
    n_sel = min(TOPK_MAX, (p + 1) // 4)
    return pl.pallas_call(
        functools.partial(_sa2_kernel, n_sel=n_sel, idx_bits=p.bit_length()),
        out_shape=[jax.ShapeDtypeStruct((nseq, 1), I32)] * 3,
        scratch_shapes=[pltpu.VMEM((ntiles, nseq, LANES), I32)],
        compiler_params=pltpu.CompilerParams(vmem_limit_bytes=VMEM_LIMIT), name="topk_threshold_sample",
    )(scores, qi, ki, wi)


def _sa3_kernel(pt_ref, tau_ref, cut_ref, new_ref, q_ref, i_ref, kvn_ref, *refs, pps, ng):
    pages, o_ref, m_s, l_s, acc_s = refs[:pps], refs[pps], refs[pps + 1], refs[pps + 2], refs[pps + 3]
    b, g = pl.program_id(0), pl.program_id(1)

    @pl.when(g == 0)
    def _():
        _init_softmax(m_s, l_s, acc_s)

    tau, cut = tau_ref[b], cut_ref[b]
    q = q_ref[...]
    lane = lax.broadcasted_iota(I32, (1, LANES), 1)
    ss = []
    for u in range(pps):
        key = _to_key(i_ref[:, u * LANES:(u + 1) * LANES])
        sel = (key > tau) | ((key == tau) & ((g * pps + u) * PAGE_SIZE + lane <= cut))
        s = jnp.dot(q, pages[u][0:LANES, :].astype(BF16), preferred_element_type=F32) * SCALE
        ss.append(s + jnp.where(sel, 0.0, NEG))
    s = jnp.concatenate(ss, axis=1)
    m_old = m_s[0]
    m_new = jnp.maximum(m_old, jnp.max(s, axis=1, keepdims=True))
    alpha = jnp.exp(m_old - m_new)
    p = jnp.exp(s - m_new)
    l_s[0] = alpha * l_s[0] + jnp.sum(p, axis=1, keepdims=True)
    pv = jnp.zeros(acc_s.shape[1:], F32)
    for u in range(pps):
        pv = pv + lax.dot_general(p[:, u * LANES:(u + 1) * LANES].astype(BF16),
                                  pages[u][LANES:2 * LANES, :].astype(BF16), _NT, preferred_element_type=F32)
    acc_s[0] = alpha * acc_s[0] + pv
    m_s[0] = m_new

    @pl.when(g == ng - 1)
    def _():
        kvn = kvn_ref[...].astype(BF16).astype(F32)
        s_new = jnp.sum(q.astype(F32) * kvn[:, 0:LANES], axis=1, keepdims=True) * SCALE
        s_new = jnp.where(new_ref[b] > 0, s_new, NEG)
        m_fin = jnp.maximum(m_s[0], s_new)
        a_fin = jnp.exp(m_s[0] - m_fin)
        p_new = jnp.exp(s_new - m_fin)
        den = a_fin * l_s[0] + p_new
        o_ref[...] = (a_fin * acc_s[0] + p_new * kvn[:, LANES:2 * LANES]) / den


def _sa3(page_table, tau, cut, new, q_blk, scores, kv_new, kv_t, layer, pps):
    nseq, npages = page_table.shape
    ng = npages // pps
    nh = q_blk.shape[1]
    gs = pltpu.PrefetchScalarGridSpec(
        num_scalar_prefetch=4, grid=(nseq, ng),
        in_specs=[pl.BlockSpec((None, nh, LANES), lambda b, g, *_: (b, 0, 0)),
                  pl.BlockSpec((None, 1, pps * LANES), lambda b, g, *_: (b, 0, g)),
                  pl.BlockSpec((None, 1, 2 * LANES), lambda b, g, *_: (b, 0, 0))] + _page_specs(2 * LANES, layer, pps),
        out_specs=pl.BlockSpec((None, nh, LANES), lambda b, g, *_: (b, 0, 0)),
        scratch_shapes=[pltpu.VMEM((1, nh, 1), F32), pltpu.VMEM((1, nh, 1), F32), pltpu.VMEM((1, nh, LANES), F32)])
    return pl.pallas_call(
        functools.partial(_sa3_kernel, pps=pps, ng=ng), grid_spec=gs,
        out_shape=jax.ShapeDtypeStruct((nseq, nh, LANES), F32),
        compiler_params=_cp("arbitrary", "arbitrary"), name="attn_a_sample",
    )(page_table, tau, cut, new, q_blk, scores, kv_new, *([kv_t] * pps))


def _sb_kernel(pt_ref, q_ref, kvn_ref, lam_ref, sg_ref, *refs, pps, ng, lam_init):
    pages, o_ref, m_s, l_s, acc_s = refs[:pps], refs[pps], refs[pps + 1], refs[pps + 2], refs[pps + 3]
    g = pl.program_id(1)
    grp = H_B // KV_B
    stride = 2 * KV_B

    @pl.when(g == 0)
    def _():
        _init_softmax(m_s, l_s, acc_s)

    for k in range(KV_B):
        q = q_ref[k]
        ss = []
        for u in range(pps):
            k_rows = pages[u][pl.ds(k, PAGE_SIZE, stride=stride), :].astype(BF16)
            ss.append(lax.dot_general(q, k_rows, _NT, preferred_element_type=F32) * SCALE)
        s = jnp.concatenate(ss, axis=1)
        m_old = m_s[k]
        m_new = jnp.maximum(m_old, jnp.max(s, axis=1, keepdims=True))
        alpha = jnp.exp(m_old - m_new)
        p = jnp.exp(s - m_new)
        l_s[k] = alpha * l_s[k] + jnp.sum(p, axis=1, keepdims=True)
        pv = jnp.zeros(acc_s.shape[1:], F32)
        for u in range(pps):
            v_rows = pages[u][pl.ds(KV_B + k, PAGE_SIZE, stride=stride), :].astype(BF16)
            pv = pv + jnp.dot(p[:, u * LANES:(u + 1) * LANES].astype(BF16), v_rows, preferred_element_type=F32)
        acc_s[k] = alpha * acc_s[k] + pv
        m_s[k] = m_new

    @pl.when(g == ng - 1)
    def _():
        lam = _lambda_value(lam_ref, lam_init)
        kvn = kvn_ref[...].astype(BF16).astype(F32)
        for k in range(KV_B):
            q = q_ref[k].astype(F32)
            s_new = jnp.sum(q * kvn[:, k * LANES:(k + 1) * LANES], axis=1, keepdims=True) * SCALE
            m_fin = jnp.maximum(m_s[k], s_new)
            a_fin = jnp.exp(m_s[k] - m_fin)
            p_new = jnp.exp(s_new - m_fin)
            den = a_fin * l_s[k] + p_new
            o = (a_fin * acc_s[k] + p_new * kvn[:, (KV_B + k) * LANES:(KV_B + k + 1) * LANES]) / den
            for gg in range(grp):
                d = o[2 * gg:2 * gg + 1, :] - lam * o[2 * gg + 1:2 * gg + 2, :]
                h = k * grp + gg
                o_ref[h:h + 1, :] = _rmsnorm_rows(d, sg_ref[...]) * (1.0 - lam_init)


def _sb(page_table, q_blk, kv_new, lam_p, subln_g, kv_v, layer, pps, lam_init):
    nseq, npages = page_table.shape
    ng = npages // pps
    rows = q_blk.shape[2]
    gs = pltpu.PrefetchScalarGridSpec(
        num_scalar_prefetch=1, grid=(nseq, ng),
        in_specs=[pl.BlockSpec((None, KV_B, rows, LANES), lambda b, g, pt: (b, 0, 0, 0)),
                  pl.BlockSpec((None, 1, 2 * KV_B * LANES), lambda b, g, pt: (b, 0, 0)),
                  pl.BlockSpec(lam_p.shape, lambda b, g, pt: (0, 0)),
                  pl.BlockSpec((1, 2 * HEAD_DIM), lambda b, g, pt: (0, 0))]
                 + _page_specs(PAGE_SIZE * 2 * KV_B, layer, pps),
        out_specs=pl.BlockSpec((None, H_B, LANES), lambda b, g, pt: (b, 0, 0)),
        scratch_shapes=[pltpu.VMEM((KV_B, rows, 1), F32), pltpu.VMEM((KV_B, rows, 1), F32),
                        pltpu.VMEM((KV_B, rows, LANES), F32)])
    return pl.pallas_call(
        functools.partial(_sb_kernel, pps=pps, ng=ng, lam_init=lam_init), grid_spec=gs,
        out_shape=jax.ShapeDtypeStruct((nseq, H_B, LANES), F32),
        compiler_params=_cp("arbitrary", "arbitrary"), name="attn_b_sample",
    )(page_table, q_blk, kv_new, lam_p, subln_g.reshape(1, 2 * HEAD_DIM), *([kv_v] * pps))


def _sc_kernel(q_ref, kvn_ref, sink_ref, page_ref, o_ref):
    q = q_ref[...]
    kvn = kvn_ref[...].astype(BF16).astype(F32)
    sink = sink_ref[...]
    s = jnp.dot(q, page_ref[0:LANES, :].astype(BF16), preferred_element_type=F32) * SCALE
    s_new = jnp.sum(q.astype(F32) * kvn[:, 0:LANES], axis=1, keepdims=True) * SCALE
    m = jnp.maximum(jnp.maximum(jnp.max(s, axis=1, keepdims=True), s_new), sink)
    p = jnp.exp(s - m)
    p_new = jnp.exp(s_new - m)
    den = jnp.sum(p, axis=1, keepdims=True) + p_new + jnp.exp(sink - m)
    pv = lax.dot_general(p.astype(BF16), page_ref[LANES:2 * LANES, :].astype(BF16), _NT, preferred_element_type=F32)
    o_ref[...] = (pv + p_new * kvn[:, LANES:2 * LANES]) / den


def _sc(q_blk, kv_new, sinks, buf_t, layer):
    nseq = q_blk.shape[0]
    return pl.pallas_call(
        _sc_kernel, grid=(nseq,),
        in_specs=[pl.BlockSpec((None, H_C, LANES), lambda b: (b, 0, 0)),
                  pl.BlockSpec((None, 1, 2 * LANES), lambda b: (b, 0, 0)),
                  pl.BlockSpec((H_C, 1), lambda b: (0, 0)),
                  pl.BlockSpec((None, None, 2 * LANES, buf_t.shape[3]), lambda b: (layer, b, 0, 0))],
        out_specs=pl.BlockSpec((None, H_C, LANES), lambda b: (b, 0, 0)),
        out_shape=jax.ShapeDtypeStruct((nseq, H_C, LANES), F32),
        compiler_params=_cp("arbitrary"), name="attn_c_sample",
    )(q_blk, kv_new, sinks.reshape(H_C, 1), buf_t)


def _slab_pad(q, n_heads, grp):
    n = q.shape[0]
    qh = q.reshape(n, n_heads // grp, grp, HEAD_DIM)
    z = jnp.zeros_like(qh)
    lo = jnp.concatenate([qh[:, 0:1], z[:, 0:1]], axis=-1)
    hi = jnp.concatenate([z[:, 1:2], qh[:, 1:2]], axis=-1)
    return jnp.concatenate([lo, hi], axis=1).reshape(n, n_heads, 2 * HEAD_DIM)


def _slab_take(o, grp):
    n, n_heads, _ = o.shape
    oh = o.reshape(n, n_heads // grp, grp, 2, HEAD_DIM)
    return jnp.concatenate([oh[:, 0:1, :, 0], oh[:, 1:2, :, 1]], axis=1).reshape(n, n_heads * HEAD_DIM)


def kernel(x_prompt, x_sample, cache_a_kv, cache_a_idx, cache_b_kv, state_c_kv, state_ffn_conv, page_table, norm_mix_g, norm_ffn_g, norm_final_g, w_in_even, w_out_even, b_lambda, b_subln_g, w_in_odd, w_out_odd, c_sinks, ffn_w_up, ffn_conv_w, ffn_conv_b, ffn_w_down):
    nb, s, d = x_prompt.shape
    nseq = x_sample.shape[0]
    npages = page_table.shape[1]
    past = npages * PAGE_SIZE
    ff = ffn_w_down.shape[1]
    wb = state_c_kv.shape[2]
    assert s % TQ == 0 and x_sample.shape[1] == 1 and wb <= WINDOW and wb == LANES
    assert KV_A == 2 and KV_B == 2 and KV_C == 2 and nseq % SUBLANES == 0

    tm_p = min(512, s)
    tm_f = min(1024, s)
    tf_p = _pick_tile(ff, 256)
    tf_s = _pick_tile(ff, 1408)
    pps = min(8, npages)
    assert npages % pps == 0

    xp = x_prompt.reshape(nb * s, d)
    xs = x_sample.reshape(nseq, d)
    cos_p, sin_p = _rope_tables(jnp.arange(s))
    cos_s, sin_s = _rope_tables(jnp.full((nseq,), past))

    idx_t = jnp.transpose(cache_a_idx, (0, 1, 3, 2))
    akv_t = jnp.transpose(cache_a_kv, (0, 1, 3, 4, 5, 2)).reshape(cache_a_kv.shape[0], cache_a_kv.shape[1], 2 * LANES, PAGE_SIZE)
    bkv_v = cache_b_kv.reshape(cache_b_kv.shape[0], cache_b_kv.shape[1], PAGE_SIZE * 2 * KV_B, 2 * HEAD_DIM)
    ckv_t = jnp.transpose(state_c_kv, (0, 1, 3, 4, 5, 2)).reshape(state_c_kv.shape[0], nseq, 2 * LANES, wb)

    akv_p, akv_s, aidx_p, aidx_s, bkv_p, bkv_s = [], [], [], [], [], []
    ckv_p, ckv_s, conv_p, conv_s = [], [], [], []
    for l in range(DEPTH):
        j = l // 2
        if l % 2 == 0:
            lam_init = 0.8 - 0.6 * math.exp(-0.3 * l)
            w_in = _even_weights(w_in_even[j])
            w_out = w_out_even[j].astype(BF16)
            wo = [w_out[:H_A * HEAD_DIM], w_out[H_A * HEAD_DIM:]]
            qa, kva, qi, ki, wi, qb, kvb = _proj(xp, norm_mix_g[l], w_in, cos_p, sin_p, _EVEN_PLAN, _EVEN_WIDTHS, tm_p)
            oa = _attn_a(qa, qi, wi, ki.astype(BF16), kva.astype(BF16), nb, s)
            ob = _attn_b(qb, kvb.astype(BF16), b_lambda[j], b_subln_g[j], lam_init, nb, s)
            xp = _outproj([oa, ob], xp, wo, tm_p)
            akv_p.append(kva.reshape(nb, s, 2, KV_A, HEAD_DIM))
            aidx_p.append(ki.reshape(nb, s, IDX_DIM))
            bkv_p.append(kvb.reshape(nb, s, 2, KV_B, 2 * HEAD_DIM))
            qa, kva, qi, ki, wi, qb, kvb = _proj(xs, norm_mix_g[l], w_in, cos_s, sin_s, _EVEN_PLAN, _EVEN_WIDTHS, nseq)
            zrow = SUBLANES - IDX_HEADS
            qi_blk = jnp.pad(qi.reshape(nseq, IDX_HEADS, IDX_DIM), ((0, 0), (0, zrow), (0, 0))).astype(BF16)
            wi_blk = jnp.pad(wi.reshape(nseq, IDX_HEADS, LANES), ((0, 0), (0, zrow), (0, 0)))
            scores = _sa1(page_table, qi_blk, wi_blk, idx_t, j, pps)
            tau, cut, new = _sa2(scores.reshape(nseq, past), qi, ki, wi)
            qa_blk = _slab_pad(qa, H_A, H_A // KV_A).astype(BF16)
            oa = _sa3(page_table, tau.reshape(nseq), cut.reshape(nseq), new.reshape(nseq), qa_blk, scores,
                      kva.reshape(nseq, 1, 2 * LANES), akv_t, j, pps)
            oa = _slab_take(oa, H_A // KV_A)
            qh = qb.reshape(nseq, KV_B, H_B // KV_B, 2, HEAD_DIM)
            z = jnp.zeros_like(qh[..., 0, :])
            qb_blk = jnp.stack([jnp.concatenate([qh[..., 0, :], z], -1), jnp.concatenate([z, qh[..., 1, :]], -1)], axis=3)
            qb_blk = qb_blk.reshape(nseq, KV_B, 2 * (H_B // KV_B), 2 * HEAD_DIM)
            qb_blk = jnp.pad(qb_blk, ((0, 0), (0, 0), (0, SUBLANES - qb_blk.shape[2]), (0, 0))).astype(BF16)
            ob = _sb(page_table, qb_blk, kvb.reshape(nseq, 1, 2 * KV_B * LANES), b_lambda[j], b_subln_g[j], bkv_v, j, pps, lam_init)
            xs = _outproj([oa, ob.reshape(nseq, H_B * 2 * HEAD_DIM)], xs, wo, nseq)
            akv_s.append(kva.reshape(nseq, 1, 2, KV_A, HEAD_DIM))
            aidx_s.append(ki.reshape(nseq, 1, IDX_DIM))
            bkv_s.append(kvb.reshape(nseq, 1, 2, KV_B, 2 * HEAD_DIM))
        else:
            w_in = w_in_odd[j].astype(BF16)
            w_out = w_out_odd[j].astype(BF16)
            q, kv = _proj(xp, norm_mix_g[l], w_in, cos_p, sin_p, _ODD_PLAN, _ODD_WIDTHS, tm_p)
            o = _attn_c(q, kv.astype(BF16), c_sinks[j], nb, s)
            xp = _outproj([o], xp, [w_out], tm_p)
            ckv_p.append(kv.reshape(nb, s, 2, KV_C, HEAD_DIM)[:, s - min(WINDOW, s):])
            q, kv = _proj(xs, norm_mix_g[l], w_in, cos_s, sin_s, _ODD_PLAN, _ODD_WIDTHS, nseq)
            q_blk = _slab_pad(q, H_C, H_C // KV_C).astype(BF16)
            o = _sc(q_blk, kv.reshape(nseq, 1, 2 * LANES), c_sinks[j], ckv_t, j)
            xs = _outproj([_slab_take(o, H_C // KV_C)], xs, [w_out], nseq)
            ckv_s.append(jnp.concatenate([state_c_kv[j][:, 1:], kv.reshape(nseq, 1, 2, KV_C, HEAD_DIM)], axis=1))
        w_up = ffn_w_up[l].astype(BF16)
        wg, wv, wd = w_up[:, :ff], w_up[:, ff:], ffn_w_down[l].astype(BF16)
        xp, tail = _ffn(xp, norm_ffn_g[l], wg, wv, ffn_conv_w[l], ffn_conv_b[l], wd, tm_f, tf_p, rows_per_seq=s)
        conv_p.append(tail.reshape(nb, s // tm_f, SUBLANES, ff)[:, -1, SUBLANES - (CONV_W - 1):, :])
        hist = state_ffn_conv[l]
        xs, g_new = _ffn(xs, norm_ffn_g[l], wg, wv, ffn_conv_w[l], ffn_conv_b[l], wd, nseq, tf_s, hist=hist)
        conv_s.append(jnp.concatenate([hist[:, 1:], g_new[:, None, :]], axis=1))
    y_prompt = _final_norm(xp, norm_final_g, tm_p).reshape(nb, s, d)
    y_sample = _final_norm(xs, norm_final_g, nseq).reshape(nseq, 1, d)
    return (y_prompt, y_sample,
            jnp.stack(akv_p), jnp.stack(akv_s), jnp.stack(aidx_p), jnp.stack(aidx_s),
            jnp.stack(bkv_p), jnp.stack(bkv_s), jnp.stack(ckv_p), jnp.stack(ckv_s),
            jnp.stack(conv_p), jnp.stack(conv_s))
```

```python
import functools
import math

import jax
import jax.numpy as jnp
from jax import lax
from jax.experimental import pallas as pl
from jax.experimental.pallas import tpu as pltpu

F32, BF16, I32 = jnp.float32, jnp.bfloat16, jnp.int32

HEAD_DIM = 64
ROPE_THETA = 10000.0
EPS = 1e-6
H_A, KV_A, IDX_HEADS, IDX_DIM, TOPK_MAX = 8, 2, 4, 64, 256
H_B, KV_B = 4, 2
H_C, KV_C, WINDOW = 16, 2, 128
CONV_W = 3
PAGE_SIZE = 128
DEPTH = 4

LANES = 128
SUBLANES = 8
VMEM_LIMIT = 56 * 1024 * 1024

TQ = 128
CH = 4
TK = CH * TQ
NEG = -1e30
PAD_KEY = -2139095041
SCALE = HEAD_DIM ** -0.5

_NT = (((1,), (1,)), ((), ()))


def _cp(*sem):
    return pltpu.CompilerParams(dimension_semantics=sem, vmem_limit_bytes=VMEM_LIMIT)


def _pick_tile(n, target):
    if n <= target:
        return n
    best = None
    for t in range(LANES, target + 1, LANES):
        if n % t == 0:
            best = t
    assert best is not None, (n, target)
    return best


def _rmsnorm_rows(x, g):
    return x * lax.rsqrt(jnp.mean(x * x, axis=-1, keepdims=True) + EPS) * g


def _to_key(x):
    b = lax.bitcast_convert_type(x, I32)
    return jnp.where(b < 0, b ^ jnp.int32(0x7FFFFFFF), b)


def _rope_tables(pos):
    half = HEAD_DIM // 2
    inv = ROPE_THETA ** (-jnp.arange(half, dtype=F32) / half)
    ang = pos.astype(F32)[:, None] * inv[None, :]
    c, s = jnp.cos(ang), jnp.sin(ang)
    return jnp.tile(c, (1, 4)), jnp.concatenate([-s, s, -s, s], axis=1)


def _proj_kernel(x_ref, g_ref, w_ref, cos_ref, sin_ref, *out_refs, plan):
    h = _rmsnorm_rows(x_ref[...], g_ref[...]).astype(BF16)
    cos, sin = cos_ref[...], sin_ref[...]
    lane = lax.broadcasted_iota(I32, cos.shape, 1)
    first_half = (lane % HEAD_DIM) < HEAD_DIM // 2
    for o_idx, col0, width, rope_w, store_w in plan:
        y = jnp.dot(h, w_ref[:, col0:col0 + width], preferred_element_type=F32)
        for c in range(width // LANES):
            yc = y[:, c * LANES:(c + 1) * LANES]
            if c * LANES < rope_w:
                partner = jnp.where(first_half, pltpu.roll(yc, LANES - HEAD_DIM // 2, 1),
                                    pltpu.roll(yc, HEAD_DIM // 2, 1))
                yc = yc * cos + partner * sin
            ow = min(LANES, store_w - c * LANES)
            out_refs[o_idx][:, c * LANES:c * LANES + ow] = yc[:, :ow]


def _proj(x, g, w, cos, sin, plan, out_widths, tm):
    m, d = x.shape
    n_pos = cos.shape[0] // tm
    return pl.pallas_call(
        functools.partial(_proj_kernel, plan=plan),
        grid=(m // tm,),
        in_specs=[pl.BlockSpec((tm, d), lambda i: (i, 0)),
                  pl.BlockSpec((1, d), lambda i: (0, 0)),
                  pl.BlockSpec(w.shape, lambda i: (0, 0)),
                  pl.BlockSpec((tm, LANES), lambda i: (i % n_pos, 0)),
                  pl.BlockSpec((tm, LANES), lambda i: (i % n_pos, 0))],
        out_specs=[pl.BlockSpec((tm, ow), lambda i: (i, 0)) for ow in out_widths],
        out_shape=[jax.ShapeDtypeStruct((m, ow), F32) for ow in out_widths],
        compiler_params=_cp("arbitrary"), name="norm_proj_rope",
    )(x, g.reshape(1, d), w, cos, sin)


def _even_weights(w):
    sizes = (H_A * HEAD_DIM, KV_A * HEAD_DIM, KV_A * HEAD_DIM, IDX_HEADS * IDX_DIM, IDX_DIM, IDX_HEADS,
             H_B * 2 * HEAD_DIM, KV_B * 2 * HEAD_DIM, KV_B * 2 * HEAD_DIM)
    offs = [0]
    for s in sizes:
        offs.append(offs[-1] + s)
    qa, ka, va, qi, ki, wi, qb, kb, vb = (w[:, offs[i]:offs[i + 1]] for i in range(9))
    ki = jnp.pad(ki, ((0, 0), (0, LANES - IDX_DIM)))
    wi = jnp.repeat(wi * (IDX_HEADS ** -0.5 * IDX_DIM ** -0.5), LANES, axis=1)
    return jnp.concatenate([qa, ka, va, qi, ki, wi, qb, kb, vb], axis=1).astype(BF16)


_EVEN_WIDTHS = (512, 256, 256, 64, 512, 512, 512)
_EVEN_PLAN = ((0, 0, 512, 512, 512), (1, 512, 256, 128, 256), (2, 768, 256, 256, 256), (3, 1024, 128, 128, 64),
              (4, 1152, 512, 0, 512), (5, 1664, 512, 512, 512), (6, 2176, 512, 256, 512))
_ODD_WIDTHS = (1024, 256)
_ODD_PLAN = ((0, 0, 1024, 1024, 1024), (1, 1024, 256, 128, 256))


def _outproj_kernel(*refs, n_in):
    o_refs, x_ref, w_refs, y_ref = refs[:n_in], refs[n_in], refs[n_in + 1:2 * n_in + 1], refs[2 * n_in + 1]
    acc = x_ref[...]
    for o_ref, w_ref in zip(o_refs, w_refs):
        acc = acc + jnp.dot(o_ref[...].astype(BF16), w_ref[...], preferred_element_type=F32)
    y_ref[...] = acc


def _outproj(os_, x, ws, tm):
    m, d = x.shape
    n = len(os_)
    return pl.pallas_call(
        functools.partial(_outproj_kernel, n_in=n),
        grid=(m // tm,),
        in_specs=([pl.BlockSpec((tm, o.shape[1]), lambda i: (i, 0)) for o in os_]
                  + [pl.BlockSpec((tm, d), lambda i: (i, 0))]
                  + [pl.BlockSpec(w.shape, lambda i: (0, 0)) for w in ws]),
        out_specs=pl.BlockSpec((tm, d), lambda i: (i, 0)),
        out_shape=jax.ShapeDtypeStruct((m, d), F32),
        compiler_params=_cp("arbitrary"), name="out_proj_residual",
    )(*os_, x, *ws)


def _final_norm_kernel(x_ref, g_ref, y_ref):
    y_ref[...] = _rmsnorm_rows(x_ref[...], g_ref[...])


def _final_norm(x, g, tm):
    m, d = x.shape
    return pl.pallas_call(
        _final_norm_kernel, grid=(m // tm,),
        in_specs=[pl.BlockSpec((tm, d), lambda i: (i, 0)), pl.BlockSpec((1, d), lambda i: (0, 0))],
        out_specs=pl.BlockSpec((tm, d), lambda i: (i, 0)),
        out_shape=jax.ShapeDtypeStruct((m, d), F32),
        compiler_params=_cp("arbitrary"), name="final_norm",
    )(x, g.reshape(1, d))


def _ffn_kernel(*refs, seq_mode, tm, rows_per_seq, nf):
    if seq_mode:
        x_ref, gn_ref, wg_ref, wv_ref, cw_ref, cb_ref, wd_ref, y_ref, tail_ref, h_s, acc_s, gs_s, carry_s = refs
    else:
        (x_ref, gn_ref, wg_ref, wv_ref, cw_ref, cb_ref, wd_ref, gm2_ref, gm1_ref,
         y_ref, gout_ref, h_s, acc_s) = refs
    m, f = pl.program_id(0), pl.program_id(1)

    @pl.when(f == 0)
    def _():
        h_s[...] = _rmsnorm_rows(x_ref[...], gn_ref[...]).astype(BF16)
        acc_s[...] = jnp.zeros_like(acc_s)

    h = h_s[...]
    g = jnp.dot(h, wg_ref[...], preferred_element_type=F32)
    v = jnp.dot(h, wv_ref[...], preferred_element_type=F32)
    if seq_mode:
        starts_seq = (m * tm) % rows_per_seq == 0

        @pl.when(starts_seq)
        def _():
            gs_s[0:SUBLANES, :] = jnp.zeros((SUBLANES, gs_s.shape[1]), F32)

        @pl.when(jnp.logical_not(starts_seq))
        def _():
            gs_s[0:SUBLANES, :] = carry_s[f]

        gs_s[SUBLANES:SUBLANES + tm, :] = g
        last = g[tm - SUBLANES:tm, :]
        carry_s[f] = last
        tail_ref[0] = last
        gm1 = gs_s[SUBLANES - 1:SUBLANES - 1 + tm, :]
        gm2 = gs_s[SUBLANES - 2:SUBLANES - 2 + tm, :]
    else:
        gm1, gm2 = gm1_ref[...], gm2_ref[...]
        gout_ref[...] = g
    cw = cw_ref[...]
    gc = cw[0:1, :] * gm2 + cw[1:2, :] * gm1 + cw[2:3, :] * g + cb_ref[...]
    act = 0.5 * gc * (1.0 + lax.erf(gc * math.sqrt(0.5))) * v
    acc_s[...] += jnp.dot(act.astype(BF16), wd_ref[...], preferred_element_type=F32)

    @pl.when(f == nf - 1)
    def _():
        y_ref[...] = x_ref[...] + acc_s[...]


def _ffn(x, gn, wg, wv, cw, cb, wd, tm, tf, rows_per_seq=None, hist=None):
    m, d = x.shape
    ff = wg.shape[1]
    nf = ff // tf
    seq_mode = hist is None
    common = [pl.BlockSpec((tm, d), lambda i, f: (i, 0)),
              pl.BlockSpec((1, d), lambda i, f: (0, 0)),
              pl.BlockSpec((d, tf), lambda i, f: (0, f)),
              pl.BlockSpec((d, tf), lambda i, f: (0, f)),
              pl.BlockSpec((CONV_W, tf), lambda i, f: (0, f)),
              pl.BlockSpec((1, tf), lambda i, f: (0, f)),
              pl.BlockSpec((tf, d), lambda i, f: (f, 0))]
    y_spec = pl.BlockSpec((tm, d), lambda i, f: (i, 0))
    y_shape = jax.ShapeDtypeStruct((m, d), F32)
    scratch = [pltpu.VMEM((tm, d), BF16), pltpu.VMEM((tm, d), F32)]
    args = [x, gn.reshape(1, d), wg, wv, cw, cb.reshape(1, ff), wd]
    if seq_mode:
        assert rows_per_seq % tm == 0 and tm >= SUBLANES
        in_specs = common
        out_specs = [y_spec, pl.BlockSpec((1, SUBLANES, tf), lambda i, f: (i, 0, f))]
        out_shape = [y_shape, jax.ShapeDtypeStruct((m // tm, SUBLANES, ff), F32)]
        scratch += [pltpu.VMEM((tm + SUBLANES, tf), F32), pltpu.VMEM((nf, SUBLANES, tf), F32)]
    else:
        in_specs = common + [pl.BlockSpec((tm, tf), lambda i, f: (i, f)), pl.BlockSpec((tm, tf), lambda i, f: (i, f))]
        out_specs = [y_spec, pl.BlockSpec((tm, tf), lambda i, f: (i, f))]
        out_shape = [y_shape, jax.ShapeDtypeStruct((m, ff), F32)]
        args += [hist[:, 0, :], hist[:, 1, :]]
    return pl.pallas_call(
        functools.partial(_ffn_kernel, seq_mode=seq_mode, tm=tm, rows_per_seq=rows_per_seq, nf=nf),
        grid=(m // tm, nf), in_specs=in_specs, out_specs=out_specs, out_shape=out_shape,
        scratch_shapes=scratch, compiler_params=_cp("arbitrary", "arbitrary"),
        name="conv_glu_seq" if seq_mode else "conv_glu_tok",
    )(*args)


def _count(key_s, nchunks, pred):
    rows = key_s.shape[1]

    def body(c, acc):
        for u in range(CH):
            t = c * CH + u
            acc = acc + jnp.where(pred(key_s[t], t), 1.0, 0.0)
        return acc

    acc = lax.fori_loop(0, nchunks, body, jnp.zeros((rows, LANES), F32))
    return jnp.sum(acc, axis=1, keepdims=True)


def _select_threshold(key_s, nchunks, kk, idx_bits):
    rows = key_s.shape[1]
    shape = (rows, LANES)
    lane = lax.broadcasted_iota(I32, shape, 1)
    c0 = _count(key_s, nchunks, lambda k, t: k >= 0)
    tau = jnp.where(c0 >= kk, jnp.int32(0), jnp.int32(-2 ** 31))

    def value_bit(i, tau):
        cand = tau + lax.shift_left(jnp.int32(1), (30 - i).astype(I32))
        cand_b = jnp.broadcast_to(cand, shape)
        c = _count(key_s, nchunks, lambda k, t: k >= cand_b)
        return jnp.where(c >= kk, cand, tau)

    tau = lax.fori_loop(0, 31, value_bit, tau)
    tau_b = jnp.broadcast_to(tau, shape)
    need = kk - _count(key_s, nchunks, lambda k, t: k > tau_b)
    if idx_bits is None:
        return tau, need

    def index_bit(i, cut):
        cand = cut + lax.shift_left(jnp.int32(1), (idx_bits - 1 - i).astype(I32))
        cand_b = jnp.broadcast_to(cand, shape)
        c = _count(key_s, nchunks, lambda k, t: (k == tau_b) & (lane < cand_b - t * LANES))
        return jnp.where(c < need, cand, cut)

    cut = lax.fori_loop(0, idx_bits, index_bit, jnp.zeros((rows, 1), I32))
    return tau, cut


def _flash_step(s, v, m_ref, l_ref, acc_ref, k):
    nt = s.shape[1] // LANES
    tiles = [s[:, t * LANES:(t + 1) * LANES] for t in range(nt)]
    smax = tiles[0]
    for t in range(1, nt):
        smax = jnp.maximum(smax, tiles[t])
    m_old = m_ref[k]
    m_new = jnp.maximum(m_old, jnp.max(smax, axis=1, keepdims=True))
    alpha = jnp.exp(m_old - m_new)
    ps = [jnp.exp(t_ - m_new) for t_ in tiles]
    lsum = ps[0]
    for t in range(1, nt):
        lsum = lsum + ps[t]
    l_ref[k] = alpha * l_ref[k] + lsum
    p = jnp.concatenate(ps, axis=1).astype(BF16)
    acc_ref[k] = alpha * acc_ref[k] + jnp.dot(p, v, preferred_element_type=F32)
    m_ref[k] = m_new


def _flash_result(l_ref, acc_ref, k):
    return acc_ref[k] / jnp.sum(l_ref[k], axis=1, keepdims=True)


def _pad_head(chunk, head, slab):
    lane = lax.broadcasted_iota(I32, chunk.shape, 1)
    if head % 2 != slab:
        chunk = pltpu.roll(chunk, HEAD_DIM, 1)
    keep = (lane < HEAD_DIM) if slab == 0 else (lane >= HEAD_DIM)
    return jnp.where(keep, chunk, 0.0)


def _gather_heads(o_ref, pieces, slabs):
    lane = lax.broadcasted_iota(I32, pieces[0].shape, 1)
    lo = lane < HEAD_DIM
    for j in range(len(pieces) // 2):
        a, b = pieces[2 * j], pieces[2 * j + 1]
        if slabs[2 * j] != 0:
            a = pltpu.roll(a, HEAD_DIM, 1)
        if slabs[2 * j + 1] != 1:
            b = pltpu.roll(b, HEAD_DIM, 1)
        o_ref[:, j * LANES:(j + 1) * LANES] = jnp.where(lo, a, b)


def _init_softmax(m_ref, l_ref, acc_ref):
    m_ref[...] = jnp.full(m_ref.shape, NEG, F32)
    l_ref[...] = jnp.zeros_like(l_ref)
    acc_ref[...] = jnp.zeros_like(acc_ref)


def _attn_a_kernel(qa_ref, qi_ref, wi_ref, ki_ref, kva_ref, o_ref, key_s, q_s, m_s, l_s, acc_s, *, n_sel):
    i = pl.program_id(1)
    nch = (i + CH) // CH
    grp = H_A // KV_A
    lane = lax.broadcasted_iota(I32, (TQ, LANES), 1)
    tpos_w = i * TQ + lax.broadcasted_iota(I32, (TQ, TK), 0)
    lane_w = lax.broadcasted_iota(I32, (TQ, TK), 1)

    qi = qi_ref[...].astype(BF16)
    qs = jnp.concatenate([qi[:, h * IDX_DIM:(h + 1) * IDX_DIM] for h in range(IDX_HEADS)], axis=0)
    ws = jnp.concatenate([wi_ref[:, h * LANES:(h + 1) * LANES] for h in range(IDX_HEADS)], axis=0)
    ws = jnp.concatenate([ws] * CH, axis=1)

    def score_chunk(c, carry):
        k_rows = ki_ref[pl.ds(pl.multiple_of(c * TK, TK), TK), :]
        s = lax.dot_general(qs, k_rows, _NT, preferred_element_type=F32)
        x = jnp.maximum(s, 0.0) * ws
        sc = x[0:TQ]
        for h in range(1, IDX_HEADS):
            sc = sc + x[h * TQ:(h + 1) * TQ]
        key = _to_key(jnp.where(c * TK + lane_w <= tpos_w, sc, -jnp.inf))
        for u in range(CH):
            key_s[c * CH + u] = key[:, u * LANES:(u + 1) * LANES]
        return carry

    lax.fori_loop(0, nch, score_chunk, 0)

    row1 = lax.broadcasted_iota(I32, (TQ, 1), 0)
    kk = jnp.minimum(n_sel, i * TQ + row1 + 1).astype(F32)
    tau, need = _select_threshold(key_s, nch, kk, None)
    tau_b = jnp.broadcast_to(tau, (TQ, LANES))
    need_b = jnp.broadcast_to(need, (TQ, LANES))
    r_i = lax.broadcasted_iota(I32, (LANES, LANES), 0)
    c_i = lax.broadcasted_iota(I32, (LANES, LANES), 1)
    prefix_m = jnp.where(r_i <= c_i, 1.0, 0.0).astype(BF16)
    total_m = jnp.ones((LANES, LANES), BF16)

    qa = qa_ref[...]
    for k in range(KV_A):
        rows = []
        for g in range(grp):
            h = k * grp + g
            rows.append(_pad_head(qa[:, (h // 2) * LANES:(h // 2 + 1) * LANES], h, k))
        q_s[k] = (jnp.concatenate(rows, axis=0) * SCALE).astype(BF16)
    _init_softmax(m_s, l_s, acc_s)

    def attend_chunk(c, ties_before):
        kv = kva_ref[pl.ds(pl.multiple_of(c * TK, TK), TK), :]
        k_slab, v_slab = kv[:, 0:LANES], kv[:, LANES:2 * LANES]
        biases = []
        for u in range(CH):
            key = key_s[c * CH + u]
            tie = key == tau_b
            tie_bf = jnp.where(tie, 1.0, 0.0).astype(BF16)
            rank = ties_before + jnp.dot(tie_bf, prefix_m, preferred_element_type=F32)
            sel = (key > tau_b) | (tie & (rank <= need_b))
            ties_before = ties_before + jnp.dot(tie_bf, total_m, preferred_element_type=F32)
            biases.append(jnp.where(sel, 0.0, NEG))
        bias = jnp.concatenate(biases, axis=1)
        for k in range(KV_A):
            s = lax.dot_general(q_s[k], k_slab, _NT, preferred_element_type=F32)
            s = (s.reshape(grp, TQ, TK) + bias[None]).reshape(grp * TQ, TK)
            _flash_step(s, v_slab, m_s, l_s, acc_s, k)
        return ties_before

    lax.fori_loop(0, nch, attend_chunk, jnp.zeros((TQ, LANES), F32))

    pieces, slabs = [], []
    for k in range(KV_A):
        o = _flash_result(l_s, acc_s, k)
        for g in range(grp):
            pieces.append(o[g * TQ:(g + 1) * TQ])
            slabs.append(k)
    _gather_heads(o_ref, pieces, slabs)


def _attn_a(qa, qi, wi, ki_bf, kva_bf, nb, s):
    nq = s // TQ
    n_sel = min(TOPK_MAX, s // 4)
    grp = H_A // KV_A
    return pl.pallas_call(
        functools.partial(_attn_a_kernel, n_sel=n_sel),
        grid=(nb, nq),
        in_specs=[pl.BlockSpec((TQ, qa.shape[1]), lambda b, i: (b * nq + i, 0)),
                  pl.BlockSpec((TQ, qi.shape[1]), lambda b, i: (b * nq + i, 0)),
                  pl.BlockSpec((TQ, wi.shape[1]), lambda b, i: (b * nq + i, 0)),
                  pl.BlockSpec((s, IDX_DIM), lambda b, i: (b, 0)),
                  pl.BlockSpec((s, 2 * LANES), lambda b, i: (b, 0))],
        out_specs=pl.BlockSpec((TQ, H_A * HEAD_DIM), lambda b, i: (b * nq + i, 0)),
        out_shape=jax.ShapeDtypeStruct((nb * s, H_A * HEAD_DIM), F32),
        scratch_shapes=[pltpu.VMEM((nq, TQ, LANES), I32),
                        pltpu.VMEM((KV_A, grp * TQ, LANES), BF16),
                        pltpu.VMEM((KV_A, grp * TQ, LANES), F32),
                        pltpu.VMEM((KV_A, grp * TQ, LANES), F32),
                        pltpu.VMEM((KV_A, grp * TQ, LANES), F32)],
        compiler_params=_cp("arbitrary", "arbitrary"), name="attn_a_prompt",
    )(qa, qi, wi, ki_bf, kva_bf)


def _lambda_value(lam_ref, lam_init):
    lp = lam_ref[...]
    a = jnp.sum(lp[0:1, :] * lp[1:2, :], axis=1, keepdims=True)
    b = jnp.sum(lp[2:3, :] * lp[3:4, :], axis=1, keepdims=True)
    return jnp.exp(a) - jnp.exp(b) + lam_init


def _attn_b_kernel(qb_ref, kvb_ref, lam_ref, sg_ref, o_ref, q_s, m_s, l_s, acc_s, *, lam_init):
    i = pl.program_id(1)
    grp = H_B // KV_B
    lane = lax.broadcasted_iota(I32, (TQ, LANES), 1)
    lo = lane < HEAD_DIM
    qb = qb_ref[...] * SCALE
    for k in range(KV_B):
        rows = []
        for g in range(grp):
            chunk = qb[:, (k * grp + g) * LANES:(k * grp + g + 1) * LANES]
            rows.append(jnp.where(lo, chunk, 0.0))
            rows.append(jnp.where(lo, 0.0, chunk))
        q_s[k] = jnp.concatenate(rows, axis=0).astype(BF16)
    _init_softmax(m_s, l_s, acc_s)

    def attend_chunk(c, bias):
        kv = kvb_ref[pl.ds(pl.multiple_of(c * TK, TK), TK), :]
        for k in range(KV_B):
            k_slab = kv[:, k * LANES:(k + 1) * LANES]
            v_slab = kv[:, (KV_B + k) * LANES:(KV_B + k + 1) * LANES]
            s = lax.dot_general(q_s[k], k_slab, _NT, preferred_element_type=F32)
            if bias is not None:
                s = (s.reshape(2 * grp, TQ, TK) + bias[None]).reshape(2 * grp * TQ, TK)
            _flash_step(s, v_slab, m_s, l_s, acc_s, k)

    n_full = i // CH

    def full_chunk(c, carry):
        attend_chunk(c, None)
        return carry

    lax.fori_loop(0, n_full, full_chunk, 0)
    tpos = i * TQ + lax.broadcasted_iota(I32, (TQ, TK), 0)
    kpos = n_full * TK + lax.broadcasted_iota(I32, (TQ, TK), 1)
    attend_chunk(n_full, jnp.where(kpos <= tpos, 0.0, NEG))

    lam = _lambda_value(lam_ref, lam_init)
    for k in range(KV_B):
        o = _flash_result(l_s, acc_s, k)
        for g in range(grp):
            d = o[(2 * g) * TQ:(2 * g + 1) * TQ] - lam * o[(2 * g + 1) * TQ:(2 * g + 2) * TQ]
            h = k * grp + g
            o_ref[:, h * LANES:(h + 1) * LANES] = _rmsnorm_rows(d, sg_ref[...]) * (1.0 - lam_init)


def _attn_b(qb, kvb_bf, lam_p, subln_g, lam_init, nb, s):
    nq = s // TQ
    grp = H_B // KV_B
    rows = 2 * grp * TQ
    return pl.pallas_call(
        functools.partial(_attn_b_kernel, lam_init=lam_init),
        grid=(nb, nq),
        in_specs=[pl.BlockSpec((TQ, qb.shape[1]), lambda b, i: (b * nq + i, 0)),
                  pl.BlockSpec((s, kvb_bf.shape[1]), lambda b, i: (b, 0)),
                  pl.BlockSpec(lam_p.shape, lambda b, i: (0, 0)),
                  pl.BlockSpec((1, 2 * HEAD_DIM), lambda b, i: (0, 0))],
        out_specs=pl.BlockSpec((TQ, H_B * 2 * HEAD_DIM), lambda b, i: (b * nq + i, 0)),
        out_shape=jax.ShapeDtypeStruct((nb * s, H_B * 2 * HEAD_DIM), F32),
        scratch_shapes=[pltpu.VMEM((KV_B, rows, LANES), BF16),
                        pltpu.VMEM((KV_B, rows, LANES), F32),
                        pltpu.VMEM((KV_B, rows, LANES), F32),
                        pltpu.VMEM((KV_B, rows, LANES), F32)],
        compiler_params=_cp("arbitrary", "arbitrary"), name="attn_b_prompt",
    )(qb, kvb_bf, lam_p, subln_g.reshape(1, 2 * HEAD_DIM))


def _attn_c_kernel(q_ref, kv_ref, sink_ref, o_ref):
    i = pl.program_id(1)
    grp = H_C // KV_C
    base = jnp.maximum(i - 1, 0) * TQ
    kv = kv_ref[pl.ds(pl.multiple_of(base, TQ), 2 * TQ), :]
    k_slab, v_slab = kv[:, 0:LANES], kv[:, LANES:2 * LANES]
    qpos = i * TQ + lax.broadcasted_iota(I32, (TQ, 2 * TQ), 0)
    kpos = base + lax.broadcasted_iota(I32, (TQ, 2 * TQ), 1)
    bias = jnp.where((kpos >= qpos - WINDOW) & (kpos <= qpos), 0.0, NEG)
    bias = jnp.concatenate([bias] * grp, axis=0)
    q = q_ref[...]
    pieces, slabs = [], []
    for k in range(KV_C):
        rows, sinks = [], []
        for g in range(grp):
            h = k * grp + g
            rows.append(_pad_head(q[:, (h // 2) * LANES:(h // 2 + 1) * LANES], h, k))
            sinks.append(jnp.broadcast_to(sink_ref[h:h + 1, :], (TQ, 1)))
        qk = jnp.concatenate(rows, axis=0).astype(BF16)
        sink = jnp.concatenate(sinks, axis=0)
        s = lax.dot_general(qk, k_slab, _NT, preferred_element_type=F32) * SCALE + bias
        m = jnp.maximum(jnp.max(s, axis=1, keepdims=True), sink)
        p = jnp.exp(s - m)
        den = jnp.sum(p, axis=1, keepdims=True) + jnp.exp(sink - m)
        o = jnp.dot(p.astype(BF16), v_slab, preferred_element_type=F32) / den
        for g in range(grp):
            pieces.append(o[g * TQ:(g + 1) * TQ])
            slabs.append(k)
    _gather_heads(o_ref, pieces, slabs)


def _attn_c(q, kv_bf, sinks, nb, s):
    nq = s // TQ
    assert nq >= 2 and TQ == WINDOW
    return pl.pallas_call(
        _attn_c_kernel, grid=(nb, nq),
        in_specs=[pl.BlockSpec((TQ, q.shape[1]), lambda b, i: (b * nq + i, 0)),
                  pl.BlockSpec((s, 2 * LANES), lambda b, i: (b, 0)),
                  pl.BlockSpec((H_C, 1), lambda b, i: (0, 0))],
        out_specs=pl.BlockSpec((TQ, H_C * HEAD_DIM), lambda b, i: (b * nq + i, 0)),
        out_shape=jax.ShapeDtypeStruct((nb * s, H_C * HEAD_DIM), F32),
        compiler_params=_cp("arbitrary", "arbitrary"), name="attn_c_prompt",
    )(q, kv_bf, sinks.reshape(H_C, 1))


def _page_specs(rows, layer, pps):
    def make(u):
        return pl.BlockSpec((None, None, rows, LANES), lambda b, g, pt, *_: (layer, pt[b, g * pps + u], 0, 0))
    return [make(u) for u in range(pps)]


def _sa1_kernel(pt_ref, q_ref, w_ref, *refs, pps):
    pages, o_ref = refs[:pps], refs[pps]
    q, w = q_ref[...], w_ref[...]
    for u in range(pps):
        s = jnp.dot(q, pages[u][...].astype(BF16), preferred_element_type=F32)
        o_ref[:, u * LANES:(u + 1) * LANES] = jnp.sum(jnp.maximum(s, 0.0) * w, axis=0, keepdims=True)


def _sa1(page_table, q_blk, w_blk, idx_t, layer, pps):
    nseq, npages = page_table.shape
    gs = pltpu.PrefetchScalarGridSpec(
        num_scalar_prefetch=1, grid=(nseq, npages // pps),
        in_specs=[pl.BlockSpec((None, SUBLANES, IDX_DIM), lambda b, g, pt: (b, 0, 0)),
                  pl.BlockSpec((None, SUBLANES, LANES), lambda b, g, pt: (b, 0, 0))] + _page_specs(IDX_DIM, layer, pps),
        out_specs=pl.BlockSpec((None, 1, pps * LANES), lambda b, g, pt: (b, 0, g)))
    return pl.pallas_call(
        functools.partial(_sa1_kernel, pps=pps), grid_spec=gs,
        out_shape=jax.ShapeDtypeStruct((nseq, 1, npages * PAGE_SIZE), F32),
        compiler_params=_cp("arbitrary", "arbitrary"), name="index_scores_sample",
    )(page_table, q_blk, w_blk, *([idx_t] * pps))


def _sa2_kernel(i_ref, qi_ref, ki_ref, wi_ref, tau_ref, cut_ref, new_ref, key_s, *, n_sel, idx_bits):
    nseq, p = i_ref.shape
    nt = p // LANES
    for t in range(nt):
        key_s[t] = _to_key(i_ref[:, t * LANES:(t + 1) * LANES])
    qi = qi_ref[...].astype(BF16).astype(F32)
    ki = ki_ref[...].astype(BF16).astype(F32)
    inew = jnp.zeros((nseq, 1), F32)
    for h in range(IDX_HEADS):
        sh = jnp.sum(qi[:, h * IDX_DIM:(h + 1) * IDX_DIM] * ki, axis=1, keepdims=True)
        inew = inew + jnp.maximum(sh, 0.0) * wi_ref[:, h * LANES:h * LANES + 1]
    lane = lax.broadcasted_iota(I32, (nseq, LANES), 1)
    key_new = _to_key(jnp.broadcast_to(inew, (nseq, LANES)))
    key_s[nt] = jnp.where(lane == 0, key_new, PAD_KEY)
    for t in range(nt + 1, key_s.shape[0]):
        key_s[t] = jnp.full((nseq, LANES), PAD_KEY, I32)
    kk = jnp.full((nseq, 1), n_sel, F32)
    tau, cut = _select_threshold(key_s, key_s.shape[0] // CH, kk, idx_bits)
    tau_ref[...] = tau
    cut_ref[...] = cut
    kn = key_new[:, 0:1]
    new_ref[...] = ((kn > tau) | ((kn == tau) & (p <= cut))).astype(I32)


def _sa2(scores, qi, ki, wi):
    nseq, p = scores.shape
    ntiles = -(-(p // LANES + 1) // CH) * CH
    n_sel = min(TOPK_MAX, (p + 1) // 4)
    return pl.pallas_call(
        functools.partial(_sa2_kernel, n_sel=n_sel, idx_bits=p.bit_length()),
        out_shape=[jax.ShapeDtypeStruct((nseq, 1), I32)] * 3,
        scratch_shapes=[pltpu.VMEM((ntiles, nseq, LANES), I32)],
        compiler_params=pltpu.CompilerParams(vmem_limit_bytes=VMEM_LIMIT), name="topk_threshold_sample",
    )(scores, qi, ki, wi)


def _sa3_kernel(pt_ref, tau_ref, cut_ref, new_ref, q_ref, i_ref, kvn_ref, *refs, pps, ng):
    pages, o_ref, m_s, l_s, acc_s = refs[:pps], refs[pps], refs[pps + 1], refs[pps + 2], refs[pps + 3]
    b, g = pl.program_id(0), pl.program_id(1)

    @pl.when(g == 0)
    def _():
        _init_softmax(m_s, l_s, acc_s)

    tau, cut = tau_ref[b], cut_ref[b]
    q = q_ref[...]
    lane = lax.broadcasted_iota(I32, (1, LANES), 1)
    ss = []
    for u in range(pps):
        key = _to_key(i_ref[:, u * LANES:(u + 1) * LANES])
        sel = (key > tau) | ((key == tau) & ((g * pps + u) * PAGE_SIZE + lane <= cut))
        s = jnp.dot(q, pages[u][0:LANES, :].astype(BF16), preferred_element_type=F32) * SCALE
        ss.append(s + jnp.where(sel, 0.0, NEG))
    s = jnp.concatenate(ss, axis=1)
    m_old = m_s[0]
    m_new = jnp.maximum(m_old, jnp.max(s, axis=1, keepdims=True))
    alpha = jnp.exp(m_old - m_new)
    p = jnp.exp(s - m_new)
    l_s[0] = alpha * l_s[0] + jnp.sum(p, axis=1, keepdims=True)
    pv = jnp.zeros(acc_s.shape[1:], F32)
    for u in range(pps):
        pv = pv + lax.dot_general(p[:, u * LANES:(u + 1) * LANES].astype(BF16),
                                  pages[u][LANES:2 * LANES, :].astype(BF16), _NT, preferred_element_type=F32)
    acc_s[0] = alpha * acc_s[0] + pv
    m_s[0] = m_new

    @pl.when(g == ng - 1)
    def _():
        kvn = kvn_ref[...].astype(BF16).astype(F32)
        s_new = jnp.sum(q.astype(F32) * kvn[:, 0:LANES], axis=1, keepdims=True) * SCALE
        s_new = jnp.where(new_ref[b] > 0, s_new, NEG)
        m_fin = jnp.maximum(m_s[0], s_new)
        a_fin = jnp.exp(m_s[0] - m_fin)
        p_new = jnp.exp(s_new - m_fin)
        den = a_fin * l_s[0] + p_new
        o_ref[...] = (a_fin * acc_s[0] + p_new * kvn[:, LANES:2 * LANES]) / den


def _sa3(page_table, tau, cut, new, q_blk, scores, kv_new, kv_t, layer, pps):
    nseq, npages = page_table.shape
    ng = npages // pps
    nh = q_blk.shape[1]
    gs = pltpu.PrefetchScalarGridSpec(
        num_scalar_prefetch=4, grid=(nseq, ng),
        in_specs=[pl.BlockSpec((None, nh, LANES), lambda b, g, *_: (b, 0, 0)),
                  pl.BlockSpec((None, 1, pps * LANES), lambda b, g, *_: (b, 0, g)),
                  pl.BlockSpec((None, 1, 2 * LANES), lambda b, g, *_: (b, 0, 0))] + _page_specs(2 * LANES, layer, pps),
        out_specs=pl.BlockSpec((None, nh, LANES), lambda b, g, *_: (b, 0, 0)),
        scratch_shapes=[pltpu.VMEM((1, nh, 1), F32), pltpu.VMEM((1, nh, 1), F32), pltpu.VMEM((1, nh, LANES), F32)])
    return pl.pallas_call(
        functools.partial(_sa3_kernel, pps=pps, ng=ng), grid_spec=gs,
        out_shape=jax.ShapeDtypeStruct((nseq, nh, LANES), F32),
        compiler_params=_cp("arbitrary", "arbitrary"), name="attn_a_sample",
    )(page_table, tau, cut, new, q_blk, scores, kv_new, *([kv_t] * pps))


def _sb_kernel(pt_ref, q_ref, kvn_ref, lam_ref, sg_ref, *refs, pps, ng, lam_init):
    pages, o_ref, m_s, l_s, acc_s = refs[:pps], refs[pps], refs[pps + 1], refs[pps + 2], refs[pps + 3]
    g = pl.program_id(1)
    grp = H_B // KV_B
    stride = 2 * KV_B

    @pl.when(g == 0)
    def _():
        _init_softmax(m_s, l_s, acc_s)

    for k in range(KV_B):
        q = q_ref[k]
        ss = []
        for u in range(pps):
            k_rows = pages[u][pl.ds(k, PAGE_SIZE, stride=stride), :].astype(BF16)
            ss.append(lax.dot_general(q, k_rows, _NT, preferred_element_type=F32) * SCALE)
        s = jnp.concatenate(ss, axis=1)
        m_old = m_s[k]
        m_new = jnp.maximum(m_old, jnp.max(s, axis=1, keepdims=True))
        alpha = jnp.exp(m_old - m_new)
        p = jnp.exp(s - m_new)
        l_s[k] = alpha * l_s[k] + jnp.sum(p, axis=1, keepdims=True)
        pv = jnp.zeros(acc_s.shape[1:], F32)
        for u in range(pps):
            v_rows = pages[u][pl.ds(KV_B + k, PAGE_SIZE, stride=stride), :].astype(BF16)
            pv = pv + jnp.dot(p[:, u * LANES:(u + 1) * LANES].astype(BF16), v_rows, preferred_element_type=F32)
        acc_s[k] = alpha * acc_s[k] + pv
        m_s[k] = m_new

    @pl.when(g == ng - 1)
    def _():
        lam = _lambda_value(lam_ref, lam_init)
        kvn = kvn_ref[...].astype(BF16).astype(F32)
        for k in range(KV_B):
            q = q_ref[k].astype(F32)
            s_new = jnp.sum(q * kvn[:, k * LANES:(k + 1) * LANES], axis=1, keepdims=True) * SCALE
            m_fin = jnp.maximum(m_s[k], s_new)
            a_fin = jnp.exp(m_s[k] - m_fin)
            p_new = jnp.exp(s_new - m_fin)
            den = a_fin * l_s[k] + p_new
            o = (a_fin * acc_s[k] + p_new * kvn[:, (KV_B + k) * LANES:(KV_B + k + 1) * LANES]) / den
            for gg in range(grp):
                d = o[2 * gg:2 * gg + 1, :] - lam * o[2 * gg + 1:2 * gg + 2, :]
                h = k * grp + gg
                o_ref[h:h + 1, :] = _rmsnorm_rows(d, sg_ref[...]) * (1.0 - lam_init)


def _sb(page_table, q_blk, kv_new, lam_p, subln_g, kv_v, layer, pps, lam_init):
    nseq, npages = page_table.shape
    ng = npages // pps
    rows = q_blk.shape[2]
    gs = pltpu.PrefetchScalarGridSpec(
        num_scalar_prefetch=1, grid=(nseq, ng),
        in_specs=[pl.BlockSpec((None, KV_B, rows, LANES), lambda b, g, pt: (b, 0, 0, 0)),
                  pl.BlockSpec((None, 1, 2 * KV_B * LANES), lambda b, g, pt: (b, 0, 0)),
                  pl.BlockSpec(lam_p.shape, lambda b, g, pt: (0, 0)),
                  pl.BlockSpec((1, 2 * HEAD_DIM), lambda b, g, pt: (0, 0))]
                 + _page_specs(PAGE_SIZE * 2 * KV_B, layer, pps),
        out_specs=pl.BlockSpec((None, H_B, LANES), lambda b, g, pt: (b, 0, 0)),
        scratch_shapes=[pltpu.VMEM((KV_B, rows, 1), F32), pltpu.VMEM((KV_B, rows, 1), F32),
                        pltpu.VMEM((KV_B, rows, LANES), F32)])
    return pl.pallas_call(
        functools.partial(_sb_kernel, pps=pps, ng=ng, lam_init=lam_init), grid_spec=gs,
        out_shape=jax.ShapeDtypeStruct((nseq, H_B, LANES), F32),
        compiler_params=_cp("arbitrary", "arbitrary"), name="attn_b_sample",
    )(page_table, q_blk, kv_new, lam_p, subln_g.reshape(1, 2 * HEAD_DIM), *([kv_v] * pps))


def _sc_kernel(q_ref, kvn_ref, sink_ref, page_ref, o_ref):
    q = q_ref[...]
    kvn = kvn_ref[...].astype(BF16).astype(F32)
    sink = sink_ref[...]
    s = jnp.dot(q, page_ref[0:LANES, :].astype(BF16), preferred_element_type=F32) * SCALE
    s_new = jnp.sum(q.astype(F32) * kvn[:, 0:LANES], axis=1, keepdims=True) * SCALE
    m = jnp.maximum(jnp.maximum(jnp.max(s, axis=1, keepdims=True), s_new), sink)
    p = jnp.exp(s - m)
    p_new = jnp.exp(s_new - m)
    den = jnp.sum(p, axis=1, keepdims=True) + p_new + jnp.exp(sink - m)
    pv = lax.dot_general(p.astype(BF16), page_ref[LANES:2 * LANES, :].astype(BF16), _NT, preferred_element_type=F32)
    o_ref[...] = (pv + p_new * kvn[:, LANES:2 * LANES]) / den


def _sc(q_blk, kv_new, sinks, buf_t, layer):
    nseq = q_blk.shape[0]
    return pl.pallas_call(
        _sc_kernel, grid=(nseq,),
        in_specs=[pl.BlockSpec((None, H_C, LANES), lambda b: (b, 0, 0)),
                  pl.BlockSpec((None, 1, 2 * LANES), lambda b: (b, 0, 0)),
                  pl.BlockSpec((H_C, 1), lambda b: (0, 0)),
                  pl.BlockSpec((None, None, 2 * LANES, buf_t.shape[3]), lambda b: (layer, b, 0, 0))],
        out_specs=pl.BlockSpec((None, H_C, LANES), lambda b: (b, 0, 0)),
        out_shape=jax.ShapeDtypeStruct((nseq, H_C, LANES), F32),
        compiler_params=_cp("arbitrary"), name="attn_c_sample",
    )(q_blk, kv_new, sinks.reshape(H_C, 1), buf_t)


def _slab_pad(q, n_heads, grp):
    n = q.shape[0]
    qh = q.reshape(n, n_heads // grp, grp, HEAD_DIM)
    z = jnp.zeros_like(qh)
    lo = jnp.concatenate([qh[:, 0:1], z[:, 0:1]], axis=-1)
    hi = jnp.concatenate([z[:, 1:2], qh[:, 1:2]], axis=-1)
    return jnp.concatenate([lo, hi], axis=1).reshape(n, n_heads, 2 * HEAD_DIM)


def _slab_take(o, grp):
    n, n_heads, _ = o.shape
    oh = o.reshape(n, n_heads // grp, grp, 2, HEAD_DIM)
    return jnp.concatenate([oh[:, 0:1, :, 0], oh[:, 1:2, :, 1]], axis=1).reshape(n, n_heads * HEAD_DIM)


def kernel(x_prompt, x_sample, cache_a_kv, cache_a_idx, cache_b_kv, state_c_kv, state_ffn_conv, page_table, norm_mix_g, norm_ffn_g, norm_final_g, w_in_even, w_out_even, b_lambda, b_subln_g, w_in_odd, w_out_odd, c_sinks, ffn_w_up, ffn_conv_w, ffn_conv_b, ffn_w_down):
    nb, s, d = x_prompt.shape
    nseq = x_sample.shape[0]
    npages = page_table.shape[1]
    past = npages * PAGE_SIZE
    ff = ffn_w_down.shape[1]
    wb = state_c_kv.shape[2]
    assert s % TK == 0 and x_sample.shape[1] == 1 and wb <= WINDOW and wb == LANES
    assert KV_A == 2 and KV_B == 2 and KV_C == 2 and nseq % SUBLANES == 0

    tm_p = min(512, s)
    tm_f = min(512, s)
    tf_p = _pick_tile(ff, 1408)
    tf_s = _pick_tile(ff, 1408)
    pps = min(32, npages)
    assert npages % pps == 0

    xp = x_prompt.reshape(nb * s, d)
    xs = x_sample.reshape(nseq, d)
    cos_p, sin_p = _rope_tables(jnp.arange(s))
    cos_s, sin_s = _rope_tables(jnp.full((nseq,), past))

    idx_t = jnp.transpose(cache_a_idx, (0, 1, 3, 2))
    akv_t = jnp.transpose(cache_a_kv, (0, 1, 3, 4, 5, 2)).reshape(cache_a_kv.shape[0], cache_a_kv.shape[1], 2 * LANES, PAGE_SIZE)
    bkv_v = cache_b_kv.reshape(cache_b_kv.shape[0], cache_b_kv.shape[1], PAGE_SIZE * 2 * KV_B, 2 * HEAD_DIM)
    ckv_t = jnp.transpose(state_c_kv, (0, 1, 3, 4, 5, 2)).reshape(state_c_kv.shape[0], nseq, 2 * LANES, wb)

    akv_p, akv_s, aidx_p, aidx_s, bkv_p, bkv_s = [], [], [], [], [], []
    ckv_p, ckv_s, conv_p, conv_s = [], [], [], []
    for l in range(DEPTH):
        j = l // 2
        if l % 2 == 0:
            lam_init = 0.8 - 0.6 * math.exp(-0.3 * l)
            w_in = _even_weights(w_in_even[j])
            w_out = w_out_even[j].astype(BF16)
            wo = [w_out[:H_A * HEAD_DIM], w_out[H_A * HEAD_DIM:]]
            qa, kva, qi, ki, wi, qb, kvb = _proj(xp, norm_mix_g[l], w_in, cos_p, sin_p, _EVEN_PLAN, _EVEN_WIDTHS, tm_p)
            oa = _attn_a(qa, qi, wi, ki.astype(BF16), kva.astype(BF16), nb, s)
            ob = _attn_b(qb, kvb.astype(BF16), b_lambda[j], b_subln_g[j], lam_init, nb, s)
            xp = _outproj([oa, ob], xp, wo, tm_p)
            akv_p.append(kva.reshape(nb, s, 2, KV_A, HEAD_DIM))
            aidx_p.append(ki.reshape(nb, s, IDX_DIM))
            bkv_p.append(kvb.reshape(nb, s, 2, KV_B, 2 * HEAD_DIM))
            qa, kva, qi, ki, wi, qb, kvb = _proj(xs, norm_mix_g[l], w_in, cos_s, sin_s, _EVEN_PLAN, _EVEN_WIDTHS, nseq)
            zrow = SUBLANES - IDX_HEADS
            qi_blk = jnp.pad(qi.reshape(nseq, IDX_HEADS, IDX_DIM), ((0, 0), (0, zrow), (0, 0))).astype(BF16)
            wi_blk = jnp.pad(wi.reshape(nseq, IDX_HEADS, LANES), ((0, 0), (0, zrow), (0, 0)))
            scores = _sa1(page_table, qi_blk, wi_blk, idx_t, j, pps)
            tau, cut, new = _sa2(scores.reshape(nseq, past), qi, ki, wi)
            qa_blk = _slab_pad(qa, H_A, H_A // KV_A).astype(BF16)
            oa = _sa3(page_table, tau.reshape(nseq), cut.reshape(nseq), new.reshape(nseq), qa_blk, scores,
                      kva.reshape(nseq, 1, 2 * LANES), akv_t, j, pps)
            oa = _slab_take(oa, H_A // KV_A)
            qh = qb.reshape(nseq, KV_B, H_B // KV_B, 2, HEAD_DIM)
            z = jnp.zeros_like(qh[..., 0, :])
            qb_blk = jnp.stack([jnp.concatenate([qh[..., 0, :], z], -1), jnp.concatenate([z, qh[..., 1, :]], -1)], axis=3)
            qb_blk = qb_blk.reshape(nseq, KV_B, 2 * (H_B // KV_B), 2 * HEAD_DIM)
            qb_blk = jnp.pad(qb_blk, ((0, 0), (0, 0), (0, SUBLANES - qb_blk.shape[2]), (0, 0))).astype(BF16)
            ob = _sb(page_table, qb_blk, kvb.reshape(nseq, 1, 2 * KV_B * LANES), b_lambda[j], b_subln_g[j], bkv_v, j, pps, lam_init)
            xs = _outproj([oa, ob.reshape(nseq, H_B * 2 * HEAD_DIM)], xs, wo, nseq)
            akv_s.append(kva.reshape(nseq, 1, 2, KV_A, HEAD_DIM))
            aidx_s.append(ki.reshape(nseq, 1, IDX_DIM))
            bkv_s.append(kvb.reshape(nseq, 1, 2, KV_B, 2 * HEAD_DIM))
        else:
            w_in = w_in_odd[j].astype(BF16)
            w_out = w_out_odd[j].astype(BF16)
            q, kv = _proj(xp, norm_mix_g[l], w_in, cos_p, sin_p, _ODD_PLAN, _ODD_WIDTHS, tm_p)
            o = _attn_c(q, kv.astype(BF16), c_sinks[j], nb, s)
            xp = _outproj([o], xp, [w_out], tm_p)
            ckv_p.append(kv.reshape(nb, s, 2, KV_C, HEAD_DIM)[:, s - min(WINDOW, s):])
            q, kv = _proj(xs, norm_mix_g[l], w_in, cos_s, sin_s, _ODD_PLAN, _ODD_WIDTHS, nseq)
            q_blk = _slab_pad(q, H_C, H_C // KV_C).astype(BF16)
            o = _sc(q_blk, kv.reshape(nseq, 1, 2 * LANES), c_sinks[j], ckv_t, j)
            xs = _outproj([_slab_take(o, H_C // KV_C)], xs, [w_out], nseq)
            ckv_s.append(jnp.concatenate([state_c_kv[j][:, 1:], kv.reshape(nseq, 1, 2, KV_C, HEAD_DIM)], axis=1))
        w_up = ffn_w_up[l].astype(BF16)
        wg, wv, wd = w_up[:, :ff], w_up[:, ff:], ffn_w_down[l].astype(BF16)
        xp, tail = _ffn(xp, norm_ffn_g[l], wg, wv, ffn_conv_w[l], ffn_conv_b[l], wd, tm_f, tf_p, rows_per_seq=s)
        conv_p.append(tail.reshape(nb, s // tm_f, SUBLANES, ff)[:, -1, SUBLANES - (CONV_W - 1):, :])
        hist = state_ffn_conv[l]
        xs, g_new = _ffn(xs, norm_ffn_g[l], wg, wv, ffn_conv_w[l], ffn_conv_b[l], wd, nseq, tf_s, hist=hist)
        conv_s.append(jnp.concatenate([hist[:, 1:], g_new[:, None, :]], axis=1))
    y_prompt = _final_norm(xp, norm_final_g, tm_p).reshape(nb, s, d)
    y_sample = _final_norm(xs, norm_final_g, nseq).reshape(nseq, 1, d)
    return (y_prompt, y_sample,
            jnp.stack(akv_p), jnp.stack(akv_s), jnp.stack(aidx_p), jnp.stack(aidx_s),
            jnp.stack(bkv_p), jnp.stack(bkv_s), jnp.stack(ckv_p), jnp.stack(ckv_s),
            jnp.stack(conv_p), jnp.stack(conv_s))
```

```python
import functools
import math

import jax
import jax.numpy as jnp
from jax import lax
from jax.experimental import pallas as pl
from jax.experimental.pallas import tpu as pltpu

F32, BF16, I32, I16 = jnp.float32, jnp.bfloat16, jnp.int32, jnp.int16

HEAD_DIM = 64
ROPE_THETA = 10000.0
EPS = 1e-6
H_A, KV_A, IDX_HEADS, IDX_DIM, TOPK_MAX = 8, 2, 4, 64, 256
H_B, KV_B = 4, 2
H_C, KV_C, WINDOW = 16, 2, 128
CONV_W = 3
PAGE_SIZE = 128
DEPTH = 4

LANES = 128
SUBLANES = 8
VMEM_LIMIT = 56 * 1024 * 1024

TQ = 128
CH = 4
TK = CH * TQ
NEG = -1e30
PAD_KEY = -2139095041
SCALE = HEAD_DIM ** -0.5

_NT = (((1,), (1,)), ((), ()))


def _cp(*sem):
    return pltpu.CompilerParams(dimension_semantics=sem, vmem_limit_bytes=VMEM_LIMIT)


def _pick_tile(n, target):
    if n <= target:
        return n
    best = None
    for t in range(LANES, target + 1, LANES):
        if n % t == 0:
            best = t
    assert best is not None, (n, target)
    return best


def _rmsnorm_rows(x, g):
    return x * lax.rsqrt(jnp.mean(x * x, axis=-1, keepdims=True) + EPS) * g


def _to_key(x):
    b = lax.bitcast_convert_type(x, I32)
    return jnp.where(b < 0, b ^ jnp.int32(0x7FFFFFFF), b)


def _rope_tables(pos):
    half = HEAD_DIM // 2
    inv = ROPE_THETA ** (-jnp.arange(half, dtype=F32) / half)
    ang = pos.astype(F32)[:, None] * inv[None, :]
    c, s = jnp.cos(ang), jnp.sin(ang)
    return jnp.tile(c, (1, 4)), jnp.concatenate([-s, s, -s, s], axis=1)


def _proj_kernel(x_ref, g_ref, w_ref, cos_ref, sin_ref, *out_refs, plan):
    h = _rmsnorm_rows(x_ref[...], g_ref[...]).astype(BF16)
    cos, sin = cos_ref[...], sin_ref[...]
    lane = lax.broadcasted_iota(I32, cos.shape, 1)
    first_half = (lane % HEAD_DIM) < HEAD_DIM // 2
    for col0, width, rope_w, store_w, o_idxs in plan:
        y = jnp.dot(h, w_ref[:, col0:col0 + width], preferred_element_type=F32)
        for c in range(width // LANES):
            yc = y[:, c * LANES:(c + 1) * LANES]
            if c * LANES < rope_w:
                partner = jnp.where(first_half, pltpu.roll(yc, LANES - HEAD_DIM // 2, 1),
                                    pltpu.roll(yc, HEAD_DIM // 2, 1))
                yc = yc * cos + partner * sin
            ow = min(LANES, store_w - c * LANES)
            for o_idx in o_idxs:
                out_refs[o_idx][:, c * LANES:c * LANES + ow] = yc[:, :ow].astype(out_refs[o_idx].dtype)


def _proj(x, g, w, cos, sin, plan, out_widths, tm):
    m, d = x.shape
    n_pos = cos.shape[0] // tm
    return pl.pallas_call(
        functools.partial(_proj_kernel, plan=plan),
        grid=(m // tm,),
        in_specs=[pl.BlockSpec((tm, d), lambda i: (i, 0)),
                  pl.BlockSpec((1, d), lambda i: (0, 0)),
                  pl.BlockSpec(w.shape, lambda i: (0, 0)),
                  pl.BlockSpec((tm, LANES), lambda i: (i % n_pos, 0)),
                  pl.BlockSpec((tm, LANES), lambda i: (i % n_pos, 0))],
        out_specs=[pl.BlockSpec((tm, ow), lambda i: (i, 0)) for ow, _ in out_widths],
        out_shape=[jax.ShapeDtypeStruct((m, ow), dt) for ow, dt in out_widths],
        compiler_params=_cp("arbitrary"), name="norm_proj_rope",
    )(x, g.reshape(1, d), w, cos, sin)


def _even_weights(w):
    sizes = (H_A * HEAD_DIM, KV_A * HEAD_DIM, KV_A * HEAD_DIM, IDX_HEADS * IDX_DIM, IDX_DIM, IDX_HEADS,
             H_B * 2 * HEAD_DIM, KV_B * 2 * HEAD_DIM, KV_B * 2 * HEAD_DIM)
    offs = [0]
    for s in sizes:
        offs.append(offs[-1] + s)
    qa, ka, va, qi, ki, wi, qb, kb, vb = (w[:, offs[i]:offs[i + 1]] for i in range(9))
    ki = jnp.pad(ki, ((0, 0), (0, LANES - IDX_DIM)))
    wi = jnp.repeat(wi * (IDX_HEADS ** -0.5 * IDX_DIM ** -0.5), LANES, axis=1)
    return jnp.concatenate([qa, ka, va, qi, ki, wi, qb, kb, vb], axis=1).astype(BF16)


_EVEN_WIDTHS = ((512, F32), (256, F32), (256, F32), (64, F32), (512, F32), (512, F32), (512, F32),
                (256, BF16), (64, BF16), (512, BF16))
_EVEN_PLAN = ((0, 512, 512, 512, (0,)), (512, 256, 128, 256, (1, 7)), (768, 256, 256, 256, (2,)),
              (1024, 128, 128, 64, (3, 8)), (1152, 512, 0, 512, (4,)), (1664, 512, 512, 512, (5,)),
              (2176, 512, 256, 512, (6, 9)))
_ODD_WIDTHS = ((1024, F32), (256, F32), (256, BF16))
_ODD_PLAN = ((0, 1024, 1024, 1024, (0,)), (1024, 256, 128, 256, (1, 2)))


def _outproj_kernel(*refs, n_in):
    o_refs, x_ref, w_refs, y_ref = refs[:n_in], refs[n_in], refs[n_in + 1:2 * n_in + 1], refs[2 * n_in + 1]
    acc = x_ref[...]
    for o_ref, w_ref in zip(o_refs, w_refs):
        acc = acc + jnp.dot(o_ref[...].astype(BF16), w_ref[...], preferred_element_type=F32)
    y_ref[...] = acc


def _outproj(os_, x, ws, tm):
    m, d = x.shape
    n = len(os_)
    return pl.pallas_call(
        functools.partial(_outproj_kernel, n_in=n),
        grid=(m // tm,),
        in_specs=([pl.BlockSpec((tm, o.shape[1]), lambda i: (i, 0)) for o in os_]
                  + [pl.BlockSpec((tm, d), lambda i: (i, 0))]
                  + [pl.BlockSpec(w.shape, lambda i: (0, 0)) for w in ws]),
        out_specs=pl.BlockSpec((tm, d), lambda i: (i, 0)),
        out_shape=jax.ShapeDtypeStruct((m, d), F32),
        compiler_params=_cp("arbitrary"), name="out_proj_residual",
    )(*os_, x, *ws)


def _final_norm_kernel(x_ref, g_ref, y_ref):
    y_ref[...] = _rmsnorm_rows(x_ref[...], g_ref[...])


def _final_norm(x, g, tm):
    m, d = x.shape
    return pl.pallas_call(
        _final_norm_kernel, grid=(m // tm,),
        in_specs=[pl.BlockSpec((tm, d), lambda i: (i, 0)), pl.BlockSpec((1, d), lambda i: (0, 0))],
        out_specs=pl.BlockSpec((tm, d), lambda i: (i, 0)),
        out_shape=jax.ShapeDtypeStruct((m, d), F32),
        compiler_params=_cp("arbitrary"), name="final_norm",
    )(x, g.reshape(1, d))


def _ffn_kernel(*refs, seq_mode, tm, rows_per_seq, nf):
    if seq_mode:
        x_ref, gn_ref, wg_ref, wv_ref, cw_ref, cb_ref, wd_ref, y_ref, tail_ref, h_s, acc_s, gs_s, carry_s = refs
    else:
        (x_ref, gn_ref, wg_ref, wv_ref, cw_ref, cb_ref, wd_ref, gm2_ref, gm1_ref,
         y_ref, gout_ref, h_s, acc_s) = refs
    m, f = pl.program_id(0), pl.program_id(1)

    @pl.when(f == 0)
    def _():
        h_s[...] = _rmsnorm_rows(x_ref[...], gn_ref[...]).astype(BF16)
        acc_s[...] = jnp.zeros_like(acc_s)

    h = h_s[...]
    g = jnp.dot(h, wg_ref[...], preferred_element_type=F32)
    v = jnp.dot(h, wv_ref[...], preferred_element_type=F32)
    if seq_mode:
        starts_seq = (m * tm) % rows_per_seq == 0

        @pl.when(starts_seq)
        def _():
            gs_s[0:SUBLANES, :] = jnp.zeros((SUBLANES, gs_s.shape[1]), F32)

        @pl.when(jnp.logical_not(starts_seq))
        def _():
            gs_s[0:SUBLANES, :] = carry_s[f]

        gs_s[SUBLANES:SUBLANES + tm, :] = g
        last = g[tm - SUBLANES:tm, :]
        carry_s[f] = last
        tail_ref[0] = last
        gm1 = gs_s[SUBLANES - 1:SUBLANES - 1 + tm, :]
        gm2 = gs_s[SUBLANES - 2:SUBLANES - 2 + tm, :]
    else:
        gm1, gm2 = gm1_ref[...], gm2_ref[...]
        gout_ref[...] = g
    cw = cw_ref[...]
    gc = cw[0:1, :] * gm2 + cw[1:2, :] * gm1 + cw[2:3, :] * g + cb_ref[...]
    act = 0.5 * gc * (1.0 + lax.erf(gc * math.sqrt(0.5))) * v
    acc_s[...] += jnp.dot(act.astype(BF16), wd_ref[...], preferred_element_type=F32)

    @pl.when(f == nf - 1)
    def _():
        y_ref[...] = x_ref[...] + acc_s[...]


def _ffn(x, gn, wg, wv, cw, cb, wd, tm, tf, rows_per_seq=None, hist=None):
    m, d = x.shape
    ff = wg.shape[1]
    nf = ff // tf
    seq_mode = hist is None
    common = [pl.BlockSpec((tm, d), lambda i, f: (i, 0)),
              pl.BlockSpec((1, d), lambda i, f: (0, 0)),
              pl.BlockSpec((d, tf), lambda i, f: (0, f)),
              pl.BlockSpec((d, tf), lambda i, f: (0, f)),
              pl.BlockSpec((CONV_W, tf), lambda i, f: (0, f)),
              pl.BlockSpec((1, tf), lambda i, f: (0, f)),
              pl.BlockSpec((tf, d), lambda i, f: (f, 0))]
    y_spec = pl.BlockSpec((tm, d), lambda i, f: (i, 0))
    y_shape = jax.ShapeDtypeStruct((m, d), F32)
    scratch = [pltpu.VMEM((tm, d), BF16), pltpu.VMEM((tm, d), F32)]
    args = [x, gn.reshape(1, d), wg, wv, cw, cb.reshape(1, ff), wd]
    if seq_mode:
        assert rows_per_seq % tm == 0 and tm >= SUBLANES
        in_specs = common
        out_specs = [y_spec, pl.BlockSpec((1, SUBLANES, tf), lambda i, f: (i, 0, f))]
        out_shape = [y_shape, jax.ShapeDtypeStruct((m // tm, SUBLANES, ff), F32)]
        scratch += [pltpu.VMEM((tm + SUBLANES, tf), F32), pltpu.VMEM((nf, SUBLANES, tf), F32)]
    else:
        in_specs = common + [pl.BlockSpec((tm, tf), lambda i, f: (i, f)), pl.BlockSpec((tm, tf), lambda i, f: (i, f))]
        out_specs = [y_spec, pl.BlockSpec((tm, tf), lambda i, f: (i, f))]
        out_shape = [y_shape, jax.ShapeDtypeStruct((m, ff), F32)]
        args += [hist[:, 0, :], hist[:, 1, :]]
    return pl.pallas_call(
        functools.partial(_ffn_kernel, seq_mode=seq_mode, tm=tm, rows_per_seq=rows_per_seq, nf=nf),
        grid=(m // tm, nf), in_specs=in_specs, out_specs=out_specs, out_shape=out_shape,
        scratch_shapes=scratch, compiler_params=_cp("arbitrary", "arbitrary"),
        name="conv_glu_seq" if seq_mode else "conv_glu_tok",
    )(*args)


def _count(key_s, nchunks, pred):
    rows = key_s.shape[1]
    per_iter = min(CH, max(1, CH * TQ // rows))
    assert CH % per_iter == 0

    def body(c, acc):
        for u in range(per_iter):
            t = c * per_iter + u
            acc = acc + jnp.where(pred(key_s[t], t), 1.0, 0.0)
        return acc

    acc = lax.fori_loop(0, nchunks * (CH // per_iter), body, jnp.zeros((rows, LANES), F32))
    return jnp.sum(acc, axis=1, keepdims=True)


def _select_threshold(key_s, nchunks, kk, idx_bits):
    rows = key_s.shape[1]
    shape = (rows, LANES)
    lane = lax.broadcasted_iota(I32, shape, 1)
    c0 = _count(key_s, nchunks, lambda k, t: k >= 0)
    tau = jnp.where(c0 >= kk, jnp.int32(0), jnp.int32(-2 ** 31))

    def value_bit(i, tau):
        cand = tau + lax.shift_left(jnp.int32(1), jnp.asarray(30 - i, I32))
        cand_b = jnp.broadcast_to(cand, shape)
        c = _count(key_s, nchunks, lambda k, t: k >= cand_b)
        return jnp.where(c >= kk, cand, tau)

    tau = lax.fori_loop(0, 31, value_bit, tau)
    tau_b = jnp.broadcast_to(tau, shape)
    need = kk - _count(key_s, nchunks, lambda k, t: k > tau_b)
    if idx_bits is None:
        return tau, need

    def index_bit(i, cut):
        cand = cut + lax.shift_left(jnp.int32(1), jnp.asarray(idx_bits - 1 - i, I32))
        cand_b = jnp.broadcast_to(cand, shape)
        c = _count(key_s, nchunks, lambda k, t: (k == tau_b) & (lane < cand_b - t * LANES))
        return jnp.where(c < need, cand, cut)

    cut = lax.fori_loop(0, idx_bits, index_bit, jnp.zeros((rows, 1), I32))
    return tau, cut


def _count16(ref, nchunks, pred):
    rows = ref.shape[1]
    per_iter = min(CH, max(1, CH * TQ // rows))
    assert CH % per_iter == 0

    def body(c, acc):
        for u in range(per_iter):
            acc = acc + jnp.where(pred(ref[c * per_iter + u]), jnp.int16(1), jnp.int16(0))
        return acc

    acc = lax.fori_loop(0, nchunks * (CH // per_iter), body, jnp.zeros((rows, LANES), I16))
    return jnp.sum(acc.astype(F32), axis=1, keepdims=True)


def _kth_largest16(ref, nchunks, kk):
    shape = (ref.shape[1], LANES)
    c0 = _count16(ref, nchunks, lambda v: v >= jnp.int16(0))
    thr = jnp.where(c0 >= kk, jnp.int32(0), jnp.int32(-2 ** 15))

    def bit(i, thr):
        cand = thr + lax.shift_left(jnp.int32(1), jnp.asarray(14 - i, I32))
        cand_b = jnp.broadcast_to(cand, shape).astype(I16)
        c = _count16(ref, nchunks, lambda v: v >= cand_b)
        return jnp.where(c >= kk, cand, thr)

    return lax.fori_loop(0, 15, bit, thr)


def _select_threshold16(hi_ref, lo_ref, nchunks, kk):
    shape = (hi_ref.shape[1], LANES)
    t_hi = _kth_largest16(hi_ref, nchunks, kk)
    t_hi_b = jnp.broadcast_to(t_hi, shape).astype(I16)
    above = _count16(hi_ref, nchunks, lambda v: v > t_hi_b)

    def keep_bucket(t, carry):
        lo_ref[t] = jnp.where(hi_ref[t] == t_hi_b, lo_ref[t], jnp.int16(-2 ** 15))
        return carry

    lax.fori_loop(0, nchunks * CH, keep_bucket, 0)
    t_lo = _kth_largest16(lo_ref, nchunks, kk - above)
    t_lo_b = jnp.broadcast_to(t_lo, shape).astype(I16)
    need = kk - above - _count16(lo_ref, nchunks, lambda v: v > t_lo_b)
    return t_hi * 65536 + (t_lo + 2 ** 15), need


def _flash_step(s, v, m_ref, l_ref, acc_ref, k):
    nt = s.shape[1] // LANES
    tiles = [s[:, t * LANES:(t + 1) * LANES] for t in range(nt)]
    smax = tiles[0]
    for t in range(1, nt):
        smax = jnp.maximum(smax, tiles[t])
    m_old = m_ref[k]
    m_new = jnp.maximum(m_old, jnp.max(smax, axis=1, keepdims=True))
    alpha = jnp.exp(m_old - m_new)
    ps = [jnp.exp(t_ - m_new) for t_ in tiles]
    if l_ref is not None:
        lsum = ps[0]
        for t in range(1, nt):
            lsum = lsum + ps[t]
        l_ref[k] = alpha * l_ref[k] + lsum
    p = jnp.concatenate(ps, axis=1).astype(BF16)
    acc_ref[k] = alpha * acc_ref[k] + jnp.dot(p, v, preferred_element_type=F32)
    m_ref[k] = m_new


def _flash_result(l_ref, acc_ref, k):
    return acc_ref[k] / jnp.sum(l_ref[k], axis=1, keepdims=True)


def _pad_head(chunk, head, slab):
    lane = lax.broadcasted_iota(I32, chunk.shape, 1)
    if head % 2 != slab:
        chunk = pltpu.roll(chunk, HEAD_DIM, 1)
    keep = (lane < HEAD_DIM) if slab == 0 else (lane >= HEAD_DIM)
    return jnp.where(keep, chunk, 0.0)


def _own_half(shape, slab):
    lane = lax.broadcasted_iota(I32, shape, 1)
    return (lane < HEAD_DIM) if slab == 0 else (lane >= HEAD_DIM)


def _with_ones(v_slab, slab):
    return jnp.where(_own_half(v_slab.shape, slab), v_slab, jnp.ones_like(v_slab))


def _gather_heads_norm(o_ref, pieces, slabs):
    lo = _own_half(pieces[0].shape, 0)
    for j in range(len(pieces) // 2):
        a, b = pieces[2 * j], pieces[2 * j + 1]
        ra, rb = pltpu.roll(a, HEAD_DIM, 1), pltpu.roll(b, HEAD_DIM, 1)
        a = a / ra if slabs[2 * j] == 0 else ra / a
        b = b / rb if slabs[2 * j + 1] == 1 else rb / b
        o_ref[:, j * LANES:(j + 1) * LANES] = jnp.where(lo, a, b)


def _init_softmax(m_ref, l_ref, acc_ref):
    m_ref[...] = jnp.full(m_ref.shape, NEG, F32)
    if l_ref is not None:
        l_ref[...] = jnp.zeros_like(l_ref)
    acc_ref[...] = jnp.zeros_like(acc_ref)


def _attn_a_kernel(qa_ref, qi_ref, wi_ref, ki_ref, kva_ref, o_ref, key_s, hi_s, lo_s, q_s, m_s, acc_s, *, n_sel, tq, sr):
    i = pl.program_id(1)
    nch = ((i + 1) * tq + TK - 1) // TK
    grp = H_A // KV_A
    tpos_w = i * tq + lax.broadcasted_iota(I32, (tq, TK), 0)
    lane_w = lax.broadcasted_iota(I32, (tq, TK), 1)

    qi = qi_ref[...].astype(BF16)
    qs = jnp.concatenate([qi[:, h * IDX_DIM:(h + 1) * IDX_DIM] for h in range(IDX_HEADS)], axis=0)
    ws = jnp.concatenate([wi_ref[:, h * LANES:(h + 1) * LANES] for h in range(IDX_HEADS)], axis=0)
    ws = jnp.concatenate([ws] * CH, axis=1)

    def score_chunk(c, carry):
        k_rows = ki_ref[pl.ds(pl.multiple_of(c * TK, TK), TK), :]
        s = lax.dot_general(qs, k_rows, _NT, preferred_element_type=F32)
        x = jnp.maximum(s, 0.0) * ws
        sc = x[0:tq]
        for h in range(1, IDX_HEADS):
            sc = sc + x[h * tq:(h + 1) * tq]
        key = _to_key(jnp.where(c * TK + lane_w <= tpos_w, sc, -jnp.inf))
        hi = lax.shift_right_arithmetic(key, 16).astype(I16)
        lo = ((key & 0xFFFF) - 2 ** 15).astype(I16)
        for u in range(CH):
            key_s[c * CH + u] = key[:, u * LANES:(u + 1) * LANES]
            hi_s[c * CH + u] = hi[:, u * LANES:(u + 1) * LANES]
            lo_s[c * CH + u] = lo[:, u * LANES:(u + 1) * LANES]
        return carry

    lax.fori_loop(0, nch, score_chunk, 0)

    taus, needs = [], []
    for r in range(tq // sr):
        row1 = lax.broadcasted_iota(I32, (sr, 1), 0)
        kk = jnp.minimum(n_sel, i * tq + r * sr + row1 + 1).astype(F32)
        nch_r = (i * tq + (r + 1) * sr + TK - 1) // TK
        tau, need = _select_threshold16(hi_s.at[:, pl.ds(r * sr, sr), :], lo_s.at[:, pl.ds(r * sr, sr), :], nch_r, kk)
        taus.append(jnp.broadcast_to(tau, (sr, LANES)))
        needs.append(jnp.broadcast_to(need, (sr, LANES)))
    tau_b = jnp.concatenate(taus, axis=0)
    need_b = jnp.concatenate(needs, axis=0)
    r_i = lax.broadcasted_iota(I32, (LANES, LANES), 0)
    c_i = lax.broadcasted_iota(I32, (LANES, LANES), 1)
    prefix_m = jnp.where(r_i <= c_i, 1.0, 0.0).astype(BF16)
    total_m = jnp.ones((LANES, LANES), BF16)

    qa = qa_ref[...]
    for k in range(KV_A):
        rows = []
        for g in range(grp):
            h = k * grp + g
            rows.append(_pad_head(qa[:, (h // 2) * LANES:(h // 2 + 1) * LANES], h, k))
        q_s[k] = (jnp.concatenate(rows, axis=0) * SCALE).astype(BF16)
    _init_softmax(m_s, None, acc_s)

    def attend_chunk(c, ties_before):
        kv = kva_ref[pl.ds(pl.multiple_of(c * TK, TK), TK), :]
        k_slab, v_slab = kv[:, 0:LANES], kv[:, LANES:2 * LANES]
        biases = []
        for u in range(CH):
            key = key_s[c * CH + u]
            tie = key == tau_b
            tie_bf = jnp.where(tie, 1.0, 0.0).astype(BF16)
            rank = ties_before + jnp.dot(tie_bf, prefix_m, preferred_element_type=F32)
            sel = (key > tau_b) | (tie & (rank <= need_b))
            ties_before = ties_before + jnp.dot(tie_bf, total_m, preferred_element_type=F32)
            biases.append(jnp.where(sel, 0.0, NEG))
        bias = jnp.concatenate(biases, axis=1)
        for k in range(KV_A):
            s = lax.dot_general(q_s[k], k_slab, _NT, preferred_element_type=F32)
            s = (s.reshape(grp, tq, TK) + bias[None]).reshape(grp * tq, TK)
            _flash_step(s, _with_ones(v_slab, k), m_s, None, acc_s, k)
        return ties_before

    lax.fori_loop(0, nch, attend_chunk, jnp.zeros((tq, LANES), F32))

    pieces, slabs = [], []
    for k in range(KV_A):
        acc = acc_s[k]
        for g in range(grp):
            pieces.append(acc[g * tq:(g + 1) * tq])
            slabs.append(k)
    _gather_heads_norm(o_ref, pieces, slabs)


def _attn_a(qa, qi, wi, ki_bf, kva_bf, nb, s, tq, sr):
    nq = s // tq
    n_sel = min(TOPK_MAX, s // 4)
    grp = H_A // KV_A
    return pl.pallas_call(
        functools.partial(_attn_a_kernel, n_sel=n_sel, tq=tq, sr=sr),
        grid=(nb, nq),
        in_specs=[pl.BlockSpec((tq, qa.shape[1]), lambda b, i: (b * nq + i, 0)),
                  pl.BlockSpec((tq, qi.shape[1]), lambda b, i: (b * nq + i, 0)),
                  pl.BlockSpec((tq, wi.shape[1]), lambda b, i: (b * nq + i, 0)),
                  pl.BlockSpec((s, IDX_DIM), lambda b, i: (b, 0)),
                  pl.BlockSpec((s, 2 * LANES), lambda b, i: (b, 0))],
        out_specs=pl.BlockSpec((tq, H_A * HEAD_DIM), lambda b, i: (b * nq + i, 0)),
        out_shape=jax.ShapeDtypeStruct((nb * s, H_A * HEAD_DIM), F32),
        scratch_shapes=[pltpu.VMEM((s // LANES, tq, LANES), I32),
                        pltpu.VMEM((s // LANES, tq, LANES), I16),
                        pltpu.VMEM((s // LANES, tq, LANES), I16),
                        pltpu.VMEM((KV_A, grp * tq, LANES), BF16),
                        pltpu.VMEM((KV_A, grp * tq, LANES), F32),
                        pltpu.VMEM((KV_A, grp * tq, LANES), F32)],
        compiler_params=_cp("arbitrary", "arbitrary"), name="attn_a_prompt",
    )(qa, qi, wi, ki_bf, kva_bf)


def _lambda_value(lam_ref, lam_init):
    lp = lam_ref[...]
    a = jnp.sum(lp[0:1, :] * lp[1:2, :], axis=1, keepdims=True)
    b = jnp.sum(lp[2:3, :] * lp[3:4, :], axis=1, keepdims=True)
    return jnp.exp(a) - jnp.exp(b) + lam_init


def _attn_b_kernel(qb_ref, kvb_ref, lam_ref, sg_ref, o_ref, q_s, m_s, l_s, acc_s, *, lam_init):
    i = pl.program_id(1)
    grp = H_B // KV_B
    lane = lax.broadcasted_iota(I32, (TQ, LANES), 1)
    lo = lane < HEAD_DIM
    qb = qb_ref[...] * SCALE
    for k in range(KV_B):
        rows = []
        for g in range(grp):
            chunk = qb[:, (k * grp + g) * LANES:(k * grp + g + 1) * LANES]
            rows.append(jnp.where(lo, chunk, 0.0))
            rows.append(jnp.where(lo, 0.0, chunk))
        q_s[k] = jnp.concatenate(rows, axis=0).astype(BF16)
    _init_softmax(m_s, l_s, acc_s)

    def attend_chunk(c, bias):
        kv = kvb_ref[pl.ds(pl.multiple_of(c * TK, TK), TK), :]
        for k in range(KV_B):
            k_slab = kv[:, k * LANES:(k + 1) * LANES]
            v_slab = kv[:, (KV_B + k) * LANES:(KV_B + k + 1) * LANES]
            s = lax.dot_general(q_s[k], k_slab, _NT, preferred_element_type=F32)
            if bias is not None:
                s = (s.reshape(2 * grp, TQ, TK) + bias[None]).reshape(2 * grp * TQ, TK)
            _flash_step(s, v_slab, m_s, l_s, acc_s, k)

    n_full = i // CH

    def full_chunk(c, carry):
        attend_chunk(c, None)
        return carry

    lax.fori_loop(0, n_full, full_chunk, 0)
    tpos = i * TQ + lax.broadcasted_iota(I32, (TQ, TK), 0)
    kpos = n_full * TK + lax.broadcasted_iota(I32, (TQ, TK), 1)
    attend_chunk(n_full, jnp.where(kpos <= tpos, 0.0, NEG))

    lam = _lambda_value(lam_ref, lam_init)
    for k in range(KV_B):
        o = _flash_result(l_s, acc_s, k)
        for g in range(grp):
            d = o[(2 * g) * TQ:(2 * g + 1) * TQ] - lam * o[(2 * g + 1) * TQ:(2 * g + 2) * TQ]
            h = k * grp + g
            o_ref[:, h * LANES:(h + 1) * LANES] = _rmsnorm_rows(d, sg_ref[...]) * (1.0 - lam_init)


def _attn_b(qb, kvb_bf, lam_p, subln_g, lam_init, nb, s):
    nq = s // TQ
    grp = H_B // KV_B
    rows = 2 * grp * TQ
    return pl.pallas_call(
        functools.partial(_attn_b_kernel, lam_init=lam_init),
        grid=(nb, nq),
        in_specs=[pl.BlockSpec((TQ, qb.shape[1]), lambda b, i: (b * nq + i, 0)),
                  pl.BlockSpec((s, kvb_bf.shape[1]), lambda b, i: (b, 0)),
                  pl.BlockSpec(lam_p.shape, lambda b, i: (0, 0)),
                  pl.BlockSpec((1, 2 * HEAD_DIM), lambda b, i: (0, 0))],
        out_specs=pl.BlockSpec((TQ, H_B * 2 * HEAD_DIM), lambda b, i: (b * nq + i, 0)),
        out_shape=jax.ShapeDtypeStruct((nb * s, H_B * 2 * HEAD_DIM), F32),
        scratch_shapes=[pltpu.VMEM((KV_B, rows, LANES), BF16),
                        pltpu.VMEM((KV_B, rows, LANES), F32),
                        pltpu.VMEM((KV_B, rows, LANES), F32),
                        pltpu.VMEM((KV_B, rows, LANES), F32)],
        compiler_params=_cp("arbitrary", "arbitrary"), name="attn_b_prompt",
    )(qb, kvb_bf, lam_p, subln_g.reshape(1, 2 * HEAD_DIM))


def _attn_c_kernel(q_ref, kv_ref, sink_ref, o_ref):
    i = pl.program_id(1)
    grp = H_C // KV_C
    base = jnp.maximum(i - 1, 0) * TQ
    kv = kv_ref[pl.ds(pl.multiple_of(base, TQ), 2 * TQ), :]
    k_slab, v_slab = kv[:, 0:LANES], kv[:, LANES:2 * LANES]
    qpos = i * TQ + lax.broadcasted_iota(I32, (TQ, 2 * TQ), 0)
    kpos = base + lax.broadcasted_iota(I32, (TQ, 2 * TQ), 1)
    bias = jnp.where((kpos >= qpos - WINDOW) & (kpos <= qpos), 0.0, NEG)
    q = q_ref[...] * SCALE
    pieces, slabs = [], []
    for k in range(KV_C):
        rows, sinks = [], []
        for g in range(grp):
            h = k * grp + g
            rows.append(_pad_head(q[:, (h // 2) * LANES:(h // 2 + 1) * LANES], h, k))
            sinks.append(jnp.broadcast_to(sink_ref[h:h + 1, :], (TQ, LANES)))
        qk = jnp.concatenate(rows, axis=0).astype(BF16)
        sink = jnp.concatenate(sinks, axis=0)
        s = lax.dot_general(qk, k_slab, _NT, preferred_element_type=F32)
        s = (s.reshape(grp, TQ, 2 * TQ) + bias[None]).reshape(grp * TQ, 2 * TQ)
        s0, s1 = s[:, 0:LANES], s[:, LANES:2 * LANES]
        m = jnp.maximum(jnp.max(jnp.maximum(s0, s1), axis=1, keepdims=True), sink)
        p = jnp.concatenate([jnp.exp(s0 - m), jnp.exp(s1 - m)], axis=1).astype(BF16)
        acc = jnp.dot(p, _with_ones(v_slab, k), preferred_element_type=F32)
        acc = acc + jnp.where(_own_half(acc.shape, k), 0.0, jnp.exp(sink - m))
        for g in range(grp):
            pieces.append(acc[g * TQ:(g + 1) * TQ])
            slabs.append(k)
    _gather_heads_norm(o_ref, pieces, slabs)


def _attn_c(q, kv_bf, sinks, nb, s):
    nq = s // TQ
    assert nq >= 2 and TQ == WINDOW
    return pl.pallas_call(
        _attn_c_kernel, grid=(nb, nq),
        in_specs=[pl.BlockSpec((TQ, q.shape[1]), lambda b, i: (b * nq + i, 0)),
                  pl.BlockSpec((s, 2 * LANES), lambda b, i: (b, 0)),
                  pl.BlockSpec((H_C, 1), lambda b, i: (0, 0))],
        out_specs=pl.BlockSpec((TQ, H_C * HEAD_DIM), lambda b, i: (b * nq + i, 0)),
        out_shape=jax.ShapeDtypeStruct((nb * s, H_C * HEAD_DIM), F32),
        compiler_params=_cp("arbitrary", "arbitrary"), name="attn_c_prompt",
    )(q, kv_bf, sinks.reshape(H_C, 1))


def _page_specs(rows, layer, pps):
    def make(u):
        return pl.BlockSpec((None, None, rows, LANES), lambda b, g, pt, *_: (layer, pt[b, g * pps + u], 0, 0))
    return [make(u) for u in range(pps)]


def _sa1_kernel(pt_ref, q_ref, w_ref, *refs, pps):
    pages, o_ref = refs[:pps], refs[pps]
    q, w = q_ref[...], w_ref[...]
    for u in range(pps):
        s = jnp.dot(q, pages[u][...].astype(BF16), preferred_element_type=F32)
        o_ref[:, u * LANES:(u + 1) * LANES] = jnp.sum(jnp.maximum(s, 0.0) * w, axis=0, keepdims=True)


def _sa1(page_table, q_blk, w_blk, idx_t, layer, pps):
    nseq, npages = page_table.shape
    gs = pltpu.PrefetchScalarGridSpec(
        num_scalar_prefetch=1, grid=(nseq, npages // pps),
        in_specs=[pl.BlockSpec((None, SUBLANES, IDX_DIM), lambda b, g, pt: (b, 0, 0)),
                  pl.BlockSpec((None, SUBLANES, LANES), lambda b, g, pt: (b, 0, 0))] + _page_specs(IDX_DIM, layer, pps),
        out_specs=pl.BlockSpec((None, 1, pps * LANES), lambda b, g, pt: (b, 0, g)))
    return pl.pallas_call(
        functools.partial(_sa1_kernel, pps=pps), grid_spec=gs,
        out_shape=jax.ShapeDtypeStruct((nseq, 1, npages * PAGE_SIZE), F32),
        compiler_params=_cp("arbitrary", "arbitrary"), name="index_scores_sample",
    )(page_table, q_blk, w_blk, *([idx_t] * pps))


def _sa2_kernel(i_ref, qi_ref, ki_ref, wi_ref, tau_ref, cut_ref, new_ref, key_s, *, n_sel, idx_bits):
    nseq, p = i_ref.shape
    nt = p // LANES
    for t in range(nt):
        key_s[t] = _to_key(i_ref[:, t * LANES:(t + 1) * LANES])
    qi = qi_ref[...].astype(BF16).astype(F32)
    ki = ki_ref[...].astype(BF16).astype(F32)
    inew = jnp.zeros((nseq, 1), F32)
    for h in range(IDX_HEADS):
        sh = jnp.sum(qi[:, h * IDX_DIM:(h + 1) * IDX_DIM] * ki, axis=1, keepdims=True)
        inew = inew + jnp.maximum(sh, 0.0) * wi_ref[:, h * LANES:h * LANES + 1]
    lane = lax.broadcasted_iota(I32, (nseq, LANES), 1)
    key_new = _to_key(jnp.broadcast_to(inew, (nseq, LANES)))
    key_s[nt] = jnp.where(lane == 0, key_new, PAD_KEY)
    for t in range(nt + 1, key_s.shape[0]):
        key_s[t] = jnp.full((nseq, LANES), PAD_KEY, I32)
    kk = jnp.full((nseq, 1), n_sel, F32)
    tau, cut = _select_threshold(key_s, key_s.shape[0] // CH, kk, idx_bits)
    tau_ref[...] = tau
    cut_ref[...] = cut
    kn = key_new[:, 0:1]
    new_ref[...] = ((kn > tau) | ((kn == tau) & (p <= cut))).astype(I32)


def _sa2(scores, qi, ki, wi):
    nseq, p = scores.shape
    ntiles = -(-(p // LANES + 1) // CH) * CH
    n_sel = min(TOPK_MAX, (p + 1) // 4)
    return pl.pallas_call(
        functools.partial(_sa2_kernel, n_sel=n_sel, idx_bits=p.bit_length()),
        out_shape=[jax.ShapeDtypeStruct((nseq, 1), I32)] * 3,
        scratch_shapes=[pltpu.VMEM((ntiles, nseq, LANES), I32)],
        compiler_params=pltpu.CompilerParams(vmem_limit_bytes=VMEM_LIMIT), name="topk_threshold_sample",
    )(scores, qi, ki, wi)


def _sa3_kernel(pt_ref, tau_ref, cut_ref, new_ref, q_ref, i_ref, kvn_ref, *refs, pps, ng):
    pages, o_ref, m_s, l_s, acc_s = refs[:pps], refs[pps], refs[pps + 1], refs[pps + 2], refs[pps + 3]
    b, g = pl.program_id(0), pl.program_id(1)

    @pl.when(g == 0)
    def _():
        _init_softmax(m_s, l_s, acc_s)

    tau, cut = tau_ref[b], cut_ref[b]
    q = q_ref[...]
    lane = lax.broadcasted_iota(I32, (1, LANES), 1)
    ss = []
    for u in range(pps):
        key = _to_key(i_ref[:, u * LANES:(u + 1) * LANES])
        sel = (key > tau) | ((key == tau) & ((g * pps + u) * PAGE_SIZE + lane <= cut))
        s = jnp.dot(q, pages[u][0:LANES, :].astype(BF16), preferred_element_type=F32) * SCALE
        ss.append(s + jnp.where(sel, 0.0, NEG))
    s = jnp.concatenate(ss, axis=1)
    m_old = m_s[0]
    m_new = jnp.maximum(m_old, jnp.max(s, axis=1, keepdims=True))
    alpha = jnp.exp(m_old - m_new)
    p = jnp.exp(s - m_new)
    l_s[0] = alpha * l_s[0] + jnp.sum(p, axis=1, keepdims=True)
    pv = jnp.zeros(acc_s.shape[1:], F32)
    for u in range(pps):
        pv = pv + lax.dot_general(p[:, u * LANES:(u + 1) * LANES].astype(BF16),
                                  pages[u][LANES:2 * LANES, :].astype(BF16), _NT, preferred_element_type=F32)
    acc_s[0] = alpha * acc_s[0] + pv
    m_s[0] = m_new

    @pl.when(g == ng - 1)
    def _():
        kvn = kvn_ref[...].astype(BF16).astype(F32)
        s_new = jnp.sum(q.astype(F32) * kvn[:, 0:LANES], axis=1, keepdims=True) * SCALE
        s_new = jnp.where(new_ref[b] > 0, s_new, NEG)
        m_fin = jnp.maximum(m_s[0], s_new)
        a_fin = jnp.exp(m_s[0] - m_fin)
        p_new = jnp.exp(s_new - m_fin)
        den = a_fin * l_s[0] + p_new
        o_ref[...] = (a_fin * acc_s[0] + p_new * kvn[:, LANES:2 * LANES]) / den


def _sa3(page_table, tau, cut, new, q_blk, scores, kv_new, kv_t, layer, pps):
    nseq, npages = page_table.shape
    ng = npages // pps
    nh = q_blk.shape[1]
    gs = pltpu.PrefetchScalarGridSpec(
        num_scalar_prefetch=4, grid=(nseq, ng),
        in_specs=[pl.BlockSpec((None, nh, LANES), lambda b, g, *_: (b, 0, 0)),
                  pl.BlockSpec((None, 1, pps * LANES), lambda b, g, *_: (b, 0, g)),
                  pl.BlockSpec((None, 1, 2 * LANES), lambda b, g, *_: (b, 0, 0))] + _page_specs(2 * LANES, layer, pps),
        out_specs=pl.BlockSpec((None, nh, LANES), lambda b, g, *_: (b, 0, 0)),
        scratch_shapes=[pltpu.VMEM((1, nh, 1), F32), pltpu.VMEM((1, nh, 1), F32), pltpu.VMEM((1, nh, LANES), F32)])
    return pl.pallas_call(
        functools.partial(_sa3_kernel, pps=pps, ng=ng), grid_spec=gs,
        out_shape=jax.ShapeDtypeStruct((nseq, nh, LANES), F32),
        compiler_params=_cp("arbitrary", "arbitrary"), name="attn_a_sample",
    )(page_table, tau, cut, new, q_blk, scores, kv_new, *([kv_t] * pps))


def _sb_kernel(pt_ref, q_ref, kvn_ref, lam_ref, sg_ref, *refs, pps, ng, lam_init):
    pages, o_ref, m_s, l_s, acc_s = refs[:pps], refs[pps], refs[pps + 1], refs[pps + 2], refs[pps + 3]
    g = pl.program_id(1)
    grp = H_B // KV_B
    stride = 2 * KV_B

    @pl.when(g == 0)
    def _():
        _init_softmax(m_s, l_s, acc_s)

    for k in range(KV_B):
        q = q_ref[k]
        ss = []
        for u in range(pps):
            k_rows = pages[u][pl.ds(k, PAGE_SIZE, stride=stride), :].astype(BF16)
            ss.append(lax.dot_general(q, k_rows, _NT, preferred_element_type=F32) * SCALE)
        s = jnp.concatenate(ss, axis=1)
        m_old = m_s[k]
        m_new = jnp.maximum(m_old, jnp.max(s, axis=1, keepdims=True))
        alpha = jnp.exp(m_old - m_new)
        p = jnp.exp(s - m_new)
        l_s[k] = alpha * l_s[k] + jnp.sum(p, axis=1, keepdims=True)
        pv = jnp.zeros(acc_s.shape[1:], F32)
        for u in range(pps):
            v_rows = pages[u][pl.ds(KV_B + k, PAGE_SIZE, stride=stride), :].astype(BF16)
            pv = pv + jnp.dot(p[:, u * LANES:(u + 1) * LANES].astype(BF16), v_rows, preferred_element_type=F32)
        acc_s[k] = alpha * acc_s[k] + pv
        m_s[k] = m_new

    @pl.when(g == ng - 1)
    def _():
        lam = _lambda_value(lam_ref, lam_init)
        kvn = kvn_ref[...].astype(BF16).astype(F32)
        for k in range(KV_B):
            q = q_ref[k].astype(F32)
            s_new = jnp.sum(q * kvn[:, k * LANES:(k + 1) * LANES], axis=1, keepdims=True) * SCALE
            m_fin = jnp.maximum(m_s[k], s_new)
            a_fin = jnp.exp(m_s[k] - m_fin)
            p_new = jnp.exp(s_new - m_fin)
            den = a_fin * l_s[k] + p_new
            o = (a_fin * acc_s[k] + p_new * kvn[:, (KV_B + k) * LANES:(KV_B + k + 1) * LANES]) / den
            for gg in range(grp):
                d = o[2 * gg:2 * gg + 1, :] - lam * o[2 * gg + 1:2 * gg + 2, :]
                h = k * grp + gg
                o_ref[h:h + 1, :] = _rmsnorm_rows(d, sg_ref[...]) * (1.0 - lam_init)


def _sb(page_table, q_blk, kv_new, lam_p, subln_g, kv_v, layer, pps, lam_init):
    nseq, npages = page_table.shape
    ng = npages // pps
    rows = q_blk.shape[2]
    gs = pltpu.PrefetchScalarGridSpec(
        num_scalar_prefetch=1, grid=(nseq, ng),
        in_specs=[pl.BlockSpec((None, KV_B, rows, LANES), lambda b, g, pt: (b, 0, 0, 0)),
                  pl.BlockSpec((None, 1, 2 * KV_B * LANES), lambda b, g, pt: (b, 0, 0)),
                  pl.BlockSpec(lam_p.shape, lambda b, g, pt: (0, 0)),
                  pl.BlockSpec((1, 2 * HEAD_DIM), lambda b, g, pt: (0, 0))]
                 + _page_specs(PAGE_SIZE * 2 * KV_B, layer, pps),
        out_specs=pl.BlockSpec((None, H_B, LANES), lambda b, g, pt: (b, 0, 0)),
        scratch_shapes=[pltpu.VMEM((KV_B, rows, 1), F32), pltpu.VMEM((KV_B, rows, 1), F32),
                        pltpu.VMEM((KV_B, rows, LANES), F32)])
    return pl.pallas_call(
        functools.partial(_sb_kernel, pps=pps, ng=ng, lam_init=lam_init), grid_spec=gs,
        out_shape=jax.ShapeDtypeStruct((nseq, H_B, LANES), F32),
        compiler_params=_cp("arbitrary", "arbitrary"), name="attn_b_sample",
    )(page_table, q_blk, kv_new, lam_p, subln_g.reshape(1, 2 * HEAD_DIM), *([kv_v] * pps))


def _sc_kernel(q_ref, kvn_ref, sink_ref, page_ref, o_ref):
    q = q_ref[...]
    kvn = kvn_ref[...].astype(BF16).astype(F32)
    sink = sink_ref[...]
    s = jnp.dot(q, page_ref[0:LANES, :].astype(BF16), preferred_element_type=F32) * SCALE
    s_new = jnp.sum(q.astype(F32) * kvn[:, 0:LANES], axis=1, keepdims=True) * SCALE
    m = jnp.maximum(jnp.maximum(jnp.max(s, axis=1, keepdims=True), s_new), sink)
    p = jnp.exp(s - m)
    p_new = jnp.exp(s_new - m)
    den = jnp.sum(p, axis=1, keepdims=True) + p_new + jnp.exp(sink - m)
    pv = lax.dot_general(p.astype(BF16), page_ref[LANES:2 * LANES, :].astype(BF16), _NT, preferred_element_type=F32)
    o_ref[...] = (pv + p_new * kvn[:, LANES:2 * LANES]) / den


def _sc(q_blk, kv_new, sinks, buf_t, layer):
    nseq = q_blk.shape[0]
    return pl.pallas_call(
        _sc_kernel, grid=(nseq,),
        in_specs=[pl.BlockSpec((None, H_C, LANES), lambda b: (b, 0, 0)),
                  pl.BlockSpec((None, 1, 2 * LANES), lambda b: (b, 0, 0)),
                  pl.BlockSpec((H_C, 1), lambda b: (0, 0)),
                  pl.BlockSpec((None, None, 2 * LANES, buf_t.shape[3]), lambda b: (layer, b, 0, 0))],
        out_specs=pl.BlockSpec((None, H_C, LANES), lambda b: (b, 0, 0)),
        out_shape=jax.ShapeDtypeStruct((nseq, H_C, LANES), F32),
        compiler_params=_cp("arbitrary"), name="attn_c_sample",
    )(q_blk, kv_new, sinks.reshape(H_C, 1), buf_t)


def _slab_pad(q, n_heads, grp):
    n = q.shape[0]
    qh = q.reshape(n, n_heads // grp, grp, HEAD_DIM)
    z = jnp.zeros_like(qh)
    lo = jnp.concatenate([qh[:, 0:1], z[:, 0:1]], axis=-1)
    hi = jnp.concatenate([z[:, 1:2], qh[:, 1:2]], axis=-1)
    return jnp.concatenate([lo, hi], axis=1).reshape(n, n_heads, 2 * HEAD_DIM)


def _slab_take(o, grp):
    n, n_heads, _ = o.shape
    oh = o.reshape(n, n_heads // grp, grp, 2, HEAD_DIM)
    return jnp.concatenate([oh[:, 0:1, :, 0], oh[:, 1:2, :, 1]], axis=1).reshape(n, n_heads * HEAD_DIM)


def kernel(x_prompt, x_sample, cache_a_kv, cache_a_idx, cache_b_kv, state_c_kv, state_ffn_conv, page_table, norm_mix_g, norm_ffn_g, norm_final_g, w_in_even, w_out_even, b_lambda, b_subln_g, w_in_odd, w_out_odd, c_sinks, ffn_w_up, ffn_conv_w, ffn_conv_b, ffn_w_down):
    nb, s, d = x_prompt.shape
    nseq = x_sample.shape[0]
    npages = page_table.shape[1]
    past = npages * PAGE_SIZE
    ff = ffn_w_down.shape[1]
    wb = state_c_kv.shape[2]
    assert s % TK == 0 and x_sample.shape[1] == 1 and wb <= WINDOW and wb == LANES
    assert KV_A == 2 and KV_B == 2 and KV_C == 2 and nseq % SUBLANES == 0

    tm_p = min(512, s)
    tm_f = min(512, s)
    tf_p = _pick_tile(ff, 1408)
    tf_s = _pick_tile(ff, 1408)
    tq_a = min(256, s)
    sr_a = min(256, s)
    pps = min(64, npages)
    assert npages % pps == 0

    xp = x_prompt.reshape(nb * s, d)
    xs = x_sample.reshape(nseq, d)
    cos_p, sin_p = _rope_tables(jnp.arange(s))
    cos_s, sin_s = _rope_tables(jnp.full((nseq,), past))

    idx_t = jnp.transpose(cache_a_idx, (0, 1, 3, 2))
    akv_t = jnp.transpose(cache_a_kv, (0, 1, 3, 4, 5, 2)).reshape(cache_a_kv.shape[0], cache_a_kv.shape[1], 2 * LANES, PAGE_SIZE)
    bkv_v = cache_b_kv.reshape(cache_b_kv.shape[0], cache_b_kv.shape[1], PAGE_SIZE * 2 * KV_B, 2 * HEAD_DIM)
    ckv_t = jnp.transpose(state_c_kv, (0, 1, 3, 4, 5, 2)).reshape(state_c_kv.shape[0], nseq, 2 * LANES, wb)

    akv_p, akv_s, aidx_p, aidx_s, bkv_p, bkv_s = [], [], [], [], [], []
    ckv_p, ckv_s, conv_p, conv_s = [], [], [], []
    for l in range(DEPTH):
        j = l // 2
        if l % 2 == 0:
            lam_init = 0.8 - 0.6 * math.exp(-0.3 * l)
            w_in = _even_weights(w_in_even[j])
            w_out = w_out_even[j].astype(BF16)
            wo = [w_out[:H_A * HEAD_DIM], w_out[H_A * HEAD_DIM:]]
            qa, kva, qi, ki, wi, qb, kvb, kva_bf, ki_bf, kvb_bf = _proj(
                xp, norm_mix_g[l], w_in, cos_p, sin_p, _EVEN_PLAN, _EVEN_WIDTHS, tm_p)
            oa = _attn_a(qa, qi, wi, ki_bf, kva_bf, nb, s, tq_a, sr_a)
            ob = _attn_b(qb, kvb_bf, b_lambda[j], b_subln_g[j], lam_init, nb, s)
            xp = _outproj([oa, ob], xp, wo, tm_p)
            akv_p.append(kva.reshape(nb, s, 2, KV_A, HEAD_DIM))
            aidx_p.append(ki.reshape(nb, s, IDX_DIM))
            bkv_p.append(kvb.reshape(nb, s, 2, KV_B, 2 * HEAD_DIM))
            qa, kva, qi, ki, wi, qb, kvb = _proj(xs, norm_mix_g[l], w_in, cos_s, sin_s, _EVEN_PLAN, _EVEN_WIDTHS, nseq)[:7]
            zrow = SUBLANES - IDX_HEADS
            qi_blk = jnp.pad(qi.reshape(nseq, IDX_HEADS, IDX_DIM), ((0, 0), (0, zrow), (0, 0))).astype(BF16)
            wi_blk = jnp.pad(wi.reshape(nseq, IDX_HEADS, LANES), ((0, 0), (0, zrow), (0, 0)))
            scores = _sa1(page_table, qi_blk, wi_blk, idx_t, j, pps)
            tau, cut, new = _sa2(scores.reshape(nseq, past), qi, ki, wi)
            qa_blk = _slab_pad(qa, H_A, H_A // KV_A).astype(BF16)
            oa = _sa3(page_table, tau.reshape(nseq), cut.reshape(nseq), new.reshape(nseq), qa_blk, scores,
                      kva.reshape(nseq, 1, 2 * LANES), akv_t, j, pps)
            oa = _slab_take(oa, H_A // KV_A)
            qh = qb.reshape(nseq, KV_B, H_B // KV_B, 2, HEAD_DIM)
            z = jnp.zeros_like(qh[..., 0, :])
            qb_blk = jnp.stack([jnp.concatenate([qh[..., 0, :], z], -1), jnp.concatenate([z, qh[..., 1, :]], -1)], axis=3)
            qb_blk = qb_blk.reshape(nseq, KV_B, 2 * (H_B // KV_B), 2 * HEAD_DIM)
            qb_blk = jnp.pad(qb_blk, ((0, 0), (0, 0), (0, SUBLANES - qb_blk.shape[2]), (0, 0))).astype(BF16)
            ob = _sb(page_table, qb_blk, kvb.reshape(nseq, 1, 2 * KV_B * LANES), b_lambda[j], b_subln_g[j], bkv_v, j, pps, lam_init)
            xs = _outproj([oa, ob.reshape(nseq, H_B * 2 * HEAD_DIM)], xs, wo, nseq)
            akv_s.append(kva.reshape(nseq, 1, 2, KV_A, HEAD_DIM))
            aidx_s.append(ki.reshape(nseq, 1, IDX_DIM))
            bkv_s.append(kvb.reshape(nseq, 1, 2, KV_B, 2 * HEAD_DIM))
        else:
            w_in = w_in_odd[j].astype(BF16)
            w_out = w_out_odd[j].astype(BF16)
            q, kv, kv_bf = _proj(xp, norm_mix_g[l], w_in, cos_p, sin_p, _ODD_PLAN, _ODD_WIDTHS, tm_p)
            o = _attn_c(q, kv_bf, c_sinks[j], nb, s)
            xp = _outproj([o], xp, [w_out], tm_p)
            ckv_p.append(kv.reshape(nb, s, 2, KV_C, HEAD_DIM)[:, s - min(WINDOW, s):])
            q, kv = _proj(xs, norm_mix_g[l], w_in, cos_s, sin_s, _ODD_PLAN, _ODD_WIDTHS, nseq)[:2]
            q_blk = _slab_pad(q, H_C, H_C // KV_C).astype(BF16)
            o = _sc(q_blk, kv.reshape(nseq, 1, 2 * LANES), c_sinks[j], ckv_t, j)
            xs = _outproj([_slab_take(o, H_C // KV_C)], xs, [w_out], nseq)
            ckv_s.append(jnp.concatenate([state_c_kv[j][:, 1:], kv.reshape(nseq, 1, 2, KV_C, HEAD_DIM)], axis=1))
        w_up = ffn_w_up[l].astype(BF16)
        wg, wv, wd = w_up[:, :ff], w_up[:, ff:], ffn_w_down[l].astype(BF16)
        xp, tail = _ffn(xp, norm_ffn_g[l], wg, wv, ffn_conv_w[l], ffn_conv_b[l], wd, tm_f, tf_p, rows_per_seq=s)
        conv_p.append(tail.reshape(nb, s // tm_f, SUBLANES, ff)[:, -1, SUBLANES - (CONV_W - 1):, :])
        hist = state_ffn_conv[l]
        xs, g_new = _ffn(xs, norm_ffn_g[l], wg, wv, ffn_conv_w[l], ffn_conv_b[l], wd, nseq, tf_s, hist=hist)
        conv_s.append(jnp.concatenate([hist[:, 1:], g_new[:, None, :]], axis=1))
    y_prompt = _final_norm(xp, norm_final_g, tm_p).reshape(nb, s, d)
    y_sample = _final_norm(xs, norm_final_g, nseq).reshape(nseq, 1, d)
    return (y_prompt, y_sample,
            jnp.stack(akv_p), jnp.stack(akv_s), jnp.stack(aidx_p), jnp.stack(aidx_s),
            jnp.stack(bkv_p), jnp.stack(bkv_s), jnp.stack(ckv_p), jnp.stack(ckv_s),
            jnp.stack(conv_p), jnp.stack(conv_s))
```

```python
import functools
import math

import jax
import jax.numpy as jnp
from jax import lax
from jax.experimental import pallas as pl
from jax.experimental.pallas import tpu as pltpu

F32, BF16, I32 = jnp.float32, jnp.bfloat16, jnp.int32

HEAD_DIM = 64
ROPE_THETA = 10000.0
EPS = 1e-6
H_A, KV_A, IDX_HEADS, IDX_DIM, TOPK_MAX = 8, 2, 4, 64, 256
H_B, KV_B = 4, 2
H_C, KV_C, WINDOW = 16, 2, 128
CONV_W = 3
PAGE_SIZE = 128
DEPTH = 4

LANES = 128
SUBLANES = 8
VMEM_LIMIT = 56 * 1024 * 1024

TQ = 128
CH = 4
TK = CH * TQ
NEG = -1e30
PAD_KEY = -2139095041
SCALE = HEAD_DIM ** -0.5

_NT = (((1,), (1,)), ((), ()))


def _cp(*sem):
    return pltpu.CompilerParams(dimension_semantics=sem, vmem_limit_bytes=VMEM_LIMIT)


def _pick_tile(n, target):
    if n <= target:
        return n
    best = None
    for t in range(LANES, target + 1, LANES):
        if n % t == 0:
            best = t
    assert best is not None, (n, target)
    return best


def _rmsnorm_rows(x, g):
    return x * lax.rsqrt(jnp.mean(x * x, axis=-1, keepdims=True) + EPS) * g


def _to_key(x):
    b = lax.bitcast_convert_type(x, I32)
    return jnp.where(b < 0, b ^ jnp.int32(0x7FFFFFFF), b)


def _rope_tables(pos):
    half = HEAD_DIM // 2
    inv = ROPE_THETA ** (-jnp.arange(half, dtype=F32) / half)
    ang = pos.astype(F32)[:, None] * inv[None, :]
    c, s = jnp.cos(ang), jnp.sin(ang)
    return jnp.tile(c, (1, 4)), jnp.concatenate([-s, s, -s, s], axis=1)


def _proj_kernel(x_ref, g_ref, w_ref, cos_ref, sin_ref, *out_refs, plan):
    h = _rmsnorm_rows(x_ref[...], g_ref[...]).astype(BF16)
    cos, sin = cos_ref[...], sin_ref[...]
    lane = lax.broadcasted_iota(I32, cos.shape, 1)
    first_half = (lane % HEAD_DIM) < HEAD_DIM // 2
    for col0, width, rope_w, store_w, o_idxs in plan:
        y = jnp.dot(h, w_ref[:, col0:col0 + width], preferred_element_type=F32)
        for c in range(width // LANES):
            yc = y[:, c * LANES:(c + 1) * LANES]
            if c * LANES < rope_w:
                partner = jnp.where(first_half, pltpu.roll(yc, LANES - HEAD_DIM // 2, 1),
                                    pltpu.roll(yc, HEAD_DIM // 2, 1))
                yc = yc * cos + partner * sin
            ow = min(LANES, store_w - c * LANES)
            for o_idx in o_idxs:
                out_refs[o_idx][:, c * LANES:c * LANES + ow] = yc[:, :ow].astype(out_refs[o_idx].dtype)


def _proj(x, g, w, cos, sin, plan, out_widths, tm):
    m, d = x.shape
    n_pos = cos.shape[0] // tm
    return pl.pallas_call(
        functools.partial(_proj_kernel, plan=plan),
        grid=(m // tm,),
        in_specs=[pl.BlockSpec((tm, d), lambda i: (i, 0)),
                  pl.BlockSpec((1, d), lambda i: (0, 0)),
                  pl.BlockSpec(w.shape, lambda i: (0, 0)),
                  pl.BlockSpec((tm, LANES), lambda i: (i % n_pos, 0)),
                  pl.BlockSpec((tm, LANES), lambda i: (i % n_pos, 0))],
        out_specs=[pl.BlockSpec((tm, ow), lambda i: (i, 0)) for ow, _ in out_widths],
        out_shape=[jax.ShapeDtypeStruct((m, ow), dt) for ow, dt in out_widths],
        compiler_params=_cp("arbitrary"), name="norm_proj_rope",
    )(x, g.reshape(1, d), w, cos, sin)


def _even_weights(w):
    sizes = (H_A * HEAD_DIM, KV_A * HEAD_DIM, KV_A * HEAD_DIM, IDX_HEADS * IDX_DIM, IDX_DIM, IDX_HEADS,
             H_B * 2 * HEAD_DIM, KV_B * 2 * HEAD_DIM, KV_B * 2 * HEAD_DIM)
    offs = [0]
    for s in sizes:
        offs.append(offs[-1] + s)
    qa, ka, va, qi, ki, wi, qb, kb, vb = (w[:, offs[i]:offs[i + 1]] for i in range(9))
    ki = jnp.pad(ki, ((0, 0), (0, LANES - IDX_DIM)))
    wi = jnp.repeat(wi * (IDX_HEADS ** -0.5 * IDX_DIM ** -0.5), LANES, axis=1)
    return jnp.concatenate([qa, ka, va, qi, ki, wi, qb, kb, vb], axis=1).astype(BF16)


_EVEN_WIDTHS = ((512, F32), (256, F32), (256, F32), (64, F32), (512, F32), (512, F32), (512, F32),
                (256, BF16), (64, BF16), (512, BF16))
_EVEN_PLAN = ((0, 512, 512, 512, (0,)), (512, 256, 128, 256, (1, 7)), (768, 256, 256, 256, (2,)),
              (1024, 128, 128, 64, (3, 8)), (1152, 512, 0, 512, (4,)), (1664, 512, 512, 512, (5,)),
              (2176, 512, 256, 512, (6, 9)))
_ODD_WIDTHS = ((1024, F32), (256, F32), (256, BF16))
_ODD_PLAN = ((0, 1024, 1024, 1024, (0,)), (1024, 256, 128, 256, (1, 2)))


def _outproj_kernel(*refs, n_in):
    o_refs, x_ref, w_refs, y_ref = refs[:n_in], refs[n_in], refs[n_in + 1:2 * n_in + 1], refs[2 * n_in + 1]
    acc = x_ref[...]
    for o_ref, w_ref in zip(o_refs, w_refs):
        acc = acc + jnp.dot(o_ref[...].astype(BF16), w_ref[...], preferred_element_type=F32)
    y_ref[...] = acc


def _outproj(os_, x, ws, tm):
    m, d = x.shape
    n = len(os_)
    return pl.pallas_call(
        functools.partial(_outproj_kernel, n_in=n),
        grid=(m // tm,),
        in_specs=([pl.BlockSpec((tm, o.shape[1]), lambda i: (i, 0)) for o in os_]
                  + [pl.BlockSpec((tm, d), lambda i: (i, 0))]
                  + [pl.BlockSpec(w.shape, lambda i: (0, 0)) for w in ws]),
        out_specs=pl.BlockSpec((tm, d), lambda i: (i, 0)),
        out_shape=jax.ShapeDtypeStruct((m, d), F32),
        compiler_params=_cp("arbitrary"), name="out_proj_residual",
    )(*os_, x, *ws)


def _final_norm_kernel(x_ref, g_ref, y_ref):
    y_ref[...] = _rmsnorm_rows(x_ref[...], g_ref[...])


def _final_norm(x, g, tm):
    m, d = x.shape
    return pl.pallas_call(
        _final_norm_kernel, grid=(m // tm,),
        in_specs=[pl.BlockSpec((tm, d), lambda i: (i, 0)), pl.BlockSpec((1, d), lambda i: (0, 0))],
        out_specs=pl.BlockSpec((tm, d), lambda i: (i, 0)),
        out_shape=jax.ShapeDtypeStruct((m, d), F32),
        compiler_params=_cp("arbitrary"), name="final_norm",
    )(x, g.reshape(1, d))


def _ffn_kernel(*refs, seq_mode, tm, rows_per_seq, nf):
    if seq_mode:
        x_ref, gn_ref, wg_ref, wv_ref, cw_ref, cb_ref, wd_ref, y_ref, tail_ref, h_s, acc_s, gs_s, carry_s = refs
    else:
        (x_ref, gn_ref, wg_ref, wv_ref, cw_ref, cb_ref, wd_ref, gm2_ref, gm1_ref,
         y_ref, gout_ref, h_s, acc_s) = refs
    m, f = pl.program_id(0), pl.program_id(1)

    @pl.when(f == 0)
    def _():
        h_s[...] = _rmsnorm_rows(x_ref[...], gn_ref[...]).astype(BF16)
        acc_s[...] = jnp.zeros_like(acc_s)

    h = h_s[...]
    g = jnp.dot(h, wg_ref[...], preferred_element_type=F32)
    v = jnp.dot(h, wv_ref[...], preferred_element_type=F32)
    if seq_mode:
        starts_seq = (m * tm) % rows_per_seq == 0

        @pl.when(starts_seq)
        def _():
            gs_s[0:SUBLANES, :] = jnp.zeros((SUBLANES, gs_s.shape[1]), F32)

        @pl.when(jnp.logical_not(starts_seq))
        def _():
            gs_s[0:SUBLANES, :] = carry_s[f]

        gs_s[SUBLANES:SUBLANES + tm, :] = g
        last = g[tm - SUBLANES:tm, :]
        carry_s[f] = last
        tail_ref[0] = last
        gm1 = gs_s[SUBLANES - 1:SUBLANES - 1 + tm, :]
        gm2 = gs_s[SUBLANES - 2:SUBLANES - 2 + tm, :]
    else:
        gm1, gm2 = gm1_ref[...], gm2_ref[...]
        gout_ref[...] = g
    cw = cw_ref[...]
    gc = cw[0:1, :] * gm2 + cw[1:2, :] * gm1 + cw[2:3, :] * g + cb_ref[...]
    act = 0.5 * gc * (1.0 + lax.erf(gc * math.sqrt(0.5))) * v
    acc_s[...] += jnp.dot(act.astype(BF16), wd_ref[...], preferred_element_type=F32)

    @pl.when(f == nf - 1)
    def _():
        y_ref[...] = x_ref[...] + acc_s[...]


def _ffn(x, gn, wg, wv, cw, cb, wd, tm, tf, rows_per_seq=None, hist=None):
    m, d = x.shape
    ff = wg.shape[1]
    nf = ff // tf
    seq_mode = hist is None
    common = [pl.BlockSpec((tm, d), lambda i, f: (i, 0)),
              pl.BlockSpec((1, d), lambda i, f: (0, 0)),
              pl.BlockSpec((d, tf), lambda i, f: (0, f)),
              pl.BlockSpec((d, tf), lambda i, f: (0, f)),
              pl.BlockSpec((CONV_W, tf), lambda i, f: (0, f)),
              pl.BlockSpec((1, tf), lambda i, f: (0, f)),
              pl.BlockSpec((tf, d), lambda i, f: (f, 0))]
    y_spec = pl.BlockSpec((tm, d), lambda i, f: (i, 0))
    y_shape = jax.ShapeDtypeStruct((m, d), F32)
    scratch = [pltpu.VMEM((tm, d), BF16), pltpu.VMEM((tm, d), F32)]
    args = [x, gn.reshape(1, d), wg, wv, cw, cb.reshape(1, ff), wd]
    if seq_mode:
        assert rows_per_seq % tm == 0 and tm >= SUBLANES
        in_specs = common
        out_specs = [y_spec, pl.BlockSpec((1, SUBLANES, tf), lambda i, f: (i, 0, f))]
        out_shape = [y_shape, jax.ShapeDtypeStruct((m // tm, SUBLANES, ff), F32)]
        scratch += [pltpu.VMEM((tm + SUBLANES, tf), F32), pltpu.VMEM((nf, SUBLANES, tf), F32)]
    else:
        in_specs = common + [pl.BlockSpec((tm, tf), lambda i, f: (i, f)), pl.BlockSpec((tm, tf), lambda i, f: (i, f))]
        out_specs = [y_spec, pl.BlockSpec((tm, tf), lambda i, f: (i, f))]
        out_shape = [y_shape, jax.ShapeDtypeStruct((m, ff), F32)]
        args += [hist[:, 0, :], hist[:, 1, :]]
    return pl.pallas_call(
        functools.partial(_ffn_kernel, seq_mode=seq_mode, tm=tm, rows_per_seq=rows_per_seq, nf=nf),
        grid=(m // tm, nf), in_specs=in_specs, out_specs=out_specs, out_shape=out_shape,
        scratch_shapes=scratch, compiler_params=_cp("arbitrary", "arbitrary"),
        name="conv_glu_seq" if seq_mode else "conv_glu_tok",
    )(*args)


def _count(key_s, nchunks, pred):
    rows = key_s.shape[1]
    per_iter = min(CH, max(1, CH * TQ // rows))
    assert CH % per_iter == 0

    def body(c, acc):
        for u in range(per_iter):
            t = c * per_iter + u
            acc = acc + jnp.where(pred(key_s[t], t), 1.0, 0.0)
        return acc

    acc = lax.fori_loop(0, nchunks * (CH // per_iter), body, jnp.zeros((rows, LANES), F32))
    return jnp.sum(acc, axis=1, keepdims=True)


def _select_threshold(key_s, nchunks, kk, idx_bits):
    rows = key_s.shape[1]
    shape = (rows, LANES)
    lane = lax.broadcasted_iota(I32, shape, 1)
    c0 = _count(key_s, nchunks, lambda k, t: k >= 0)
    tau = jnp.where(c0 >= kk, jnp.int32(0), jnp.int32(-2 ** 31))

    def value_bit(i, tau):
        cand = tau + lax.shift_left(jnp.int32(1), jnp.asarray(30 - i, I32))
        cand_b = jnp.broadcast_to(cand, shape)
        c = _count(key_s, nchunks, lambda k, t: k >= cand_b)
        return jnp.where(c >= kk, cand, tau)

    tau = lax.fori_loop(0, 31, value_bit, tau)
    tau_b = jnp.broadcast_to(tau, shape)
    need = kk - _count(key_s, nchunks, lambda k, t: k > tau_b)
    if idx_bits is None:
        return tau, need

    def index_bit(i, cut):
        cand = cut + lax.shift_left(jnp.int32(1), jnp.asarray(idx_bits - 1 - i, I32))
        cand_b = jnp.broadcast_to(cand, shape)
        c = _count(key_s, nchunks, lambda k, t: (k == tau_b) & (lane < cand_b - t * LANES))
        return jnp.where(c < need, cand, cut)

    cut = lax.fori_loop(0, idx_bits, index_bit, jnp.zeros((rows, 1), I32))
    return tau, cut


def _flash_step(s, v, m_ref, l_ref, acc_ref, k):
    nt = s.shape[1] // LANES
    tiles = [s[:, t * LANES:(t + 1) * LANES] for t in range(nt)]
    smax = tiles[0]
    for t in range(1, nt):
        smax = jnp.maximum(smax, tiles[t])
    m_old = m_ref[k]
    m_new = jnp.maximum(m_old, jnp.max(smax, axis=1, keepdims=True))
    alpha = jnp.exp(m_old - m_new)
    ps = [jnp.exp(t_ - m_new) for t_ in tiles]
    if l_ref is not None:
        lsum = ps[0]
        for t in range(1, nt):
            lsum = lsum + ps[t]
        l_ref[k] = alpha * l_ref[k] + lsum
    p = jnp.concatenate(ps, axis=1).astype(BF16)
    acc_ref[k] = alpha * acc_ref[k] + jnp.dot(p, v, preferred_element_type=F32)
    m_ref[k] = m_new


def _flash_result(l_ref, acc_ref, k):
    return acc_ref[k] / jnp.sum(l_ref[k], axis=1, keepdims=True)


def _pad_head(chunk, head, slab):
    lane = lax.broadcasted_iota(I32, chunk.shape, 1)
    if head % 2 != slab:
        chunk = pltpu.roll(chunk, HEAD_DIM, 1)
    keep = (lane < HEAD_DIM) if slab == 0 else (lane >= HEAD_DIM)
    return jnp.where(keep, chunk, 0.0)


def _own_half(shape, slab):
    lane = lax.broadcasted_iota(I32, shape, 1)
    return (lane < HEAD_DIM) if slab == 0 else (lane >= HEAD_DIM)


def _with_ones(v_slab, slab):
    return jnp.where(_own_half(v_slab.shape, slab), v_slab, jnp.ones_like(v_slab))


def _gather_heads_norm(o_ref, pieces, slabs):
    lo = _own_half(pieces[0].shape, 0)
    for j in range(len(pieces) // 2):
        a, b = pieces[2 * j], pieces[2 * j + 1]
        ra, rb = pltpu.roll(a, HEAD_DIM, 1), pltpu.roll(b, HEAD_DIM, 1)
        a = a / ra if slabs[2 * j] == 0 else ra / a
        b = b / rb if slabs[2 * j + 1] == 1 else rb / b
        o_ref[:, j * LANES:(j + 1) * LANES] = jnp.where(lo, a, b)


def _init_softmax(m_ref, l_ref, acc_ref):
    m_ref[...] = jnp.full(m_ref.shape, NEG, F32)
    if l_ref is not None:
        l_ref[...] = jnp.zeros_like(l_ref)
    acc_ref[...] = jnp.zeros_like(acc_ref)


def _attn_a_kernel(qa_ref, qi_ref, wi_ref, ki_ref, kva_ref, o_ref, key_s, q_s, m_s, acc_s, *, n_sel, tq):
    i = pl.program_id(1)
    nch = ((i + 1) * tq + TK - 1) // TK
    grp = H_A // KV_A
    tpos_w = i * tq + lax.broadcasted_iota(I32, (tq, TK), 0)
    lane_w = lax.broadcasted_iota(I32, (tq, TK), 1)

    qi = qi_ref[...].astype(BF16)
    qs = jnp.concatenate([qi[:, h * IDX_DIM:(h + 1) * IDX_DIM] for h in range(IDX_HEADS)], axis=0)
    ws = jnp.concatenate([wi_ref[:, h * LANES:(h + 1) * LANES] for h in range(IDX_HEADS)], axis=0)
    ws = jnp.concatenate([ws] * CH, axis=1)

    def score_chunk(c, carry):
        k_rows = ki_ref[pl.ds(pl.multiple_of(c * TK, TK), TK), :]
        s = lax.dot_general(qs, k_rows, _NT, preferred_element_type=F32)
        x = jnp.maximum(s, 0.0) * ws
        sc = x[0:tq]
        for h in range(1, IDX_HEADS):
            sc = sc + x[h * tq:(h + 1) * tq]
        key = _to_key(jnp.where(c * TK + lane_w <= tpos_w, sc, -jnp.inf))
        for u in range(CH):
            key_s[c * CH + u] = key[:, u * LANES:(u + 1) * LANES]
        return carry

    lax.fori_loop(0, nch, score_chunk, 0)

    taus, needs = [], []
    for r in range(tq // TQ):
        row1 = lax.broadcasted_iota(I32, (TQ, 1), 0)
        kk = jnp.minimum(n_sel, i * tq + r * TQ + row1 + 1).astype(F32)
        nch_r = (i * tq + (r + 1) * TQ + TK - 1) // TK
        tau, need = _select_threshold(key_s.at[:, pl.ds(r * TQ, TQ), :], nch_r, kk, None)
        taus.append(jnp.broadcast_to(tau, (TQ, LANES)))
        needs.append(jnp.broadcast_to(need, (TQ, LANES)))
    tau_b = jnp.concatenate(taus, axis=0)
    need_b = jnp.concatenate(needs, axis=0)
    r_i = lax.broadcasted_iota(I32, (LANES, LANES), 0)
    c_i = lax.broadcasted_iota(I32, (LANES, LANES), 1)
    prefix_m = jnp.where(r_i <= c_i, 1.0, 0.0).astype(BF16)
    total_m = jnp.ones((LANES, LANES), BF16)

    qa = qa_ref[...]
    for k in range(KV_A):
        rows = []
        for g in range(grp):
            h = k * grp + g
            rows.append(_pad_head(qa[:, (h // 2) * LANES:(h // 2 + 1) * LANES], h, k))
        q_s[k] = (jnp.concatenate(rows, axis=0) * SCALE).astype(BF16)
    _init_softmax(m_s, None, acc_s)

    def attend_chunk(c, ties_before):
        kv = kva_ref[pl.ds(pl.multiple_of(c * TK, TK), TK), :]
        k_slab, v_slab = kv[:, 0:LANES], kv[:, LANES:2 * LANES]
        biases = []
        for u in range(CH):
            key = key_s[c * CH + u]
            tie = key == tau_b
            tie_bf = jnp.where(tie, 1.0, 0.0).astype(BF16)
            rank = ties_before + jnp.dot(tie_bf, prefix_m, preferred_element_type=F32)
            sel = (key > tau_b) | (tie & (rank <= need_b))
            ties_before = ties_before + jnp.dot(tie_bf, total_m, preferred_element_type=F32)
            biases.append(jnp.where(sel, 0.0, NEG))
        bias = jnp.concatenate(biases, axis=1)
        for k in range(KV_A):
            s = lax.dot_general(q_s[k], k_slab, _NT, preferred_element_type=F32)
            s = (s.reshape(grp, tq, TK) + bias[None]).reshape(grp * tq, TK)
            _flash_step(s, _with_ones(v_slab, k), m_s, None, acc_s, k)
        return ties_before

    lax.fori_loop(0, nch, attend_chunk, jnp.zeros((tq, LANES), F32))

    pieces, slabs = [], []
    for k in range(KV_A):
        acc = acc_s[k]
        for g in range(grp):
            pieces.append(acc[g * tq:(g + 1) * tq])
            slabs.append(k)
    _gather_heads_norm(o_ref, pieces, slabs)


def _attn_a(qa, qi, wi, ki_bf, kva_bf, nb, s, tq):
    nq = s // tq
    n_sel = min(TOPK_MAX, s // 4)
    grp = H_A // KV_A
    return pl.pallas_call(
        functools.partial(_attn_a_kernel, n_sel=n_sel, tq=tq),
        grid=(nb, nq),
        in_specs=[pl.BlockSpec((tq, qa.shape[1]), lambda b, i: (b * nq + i, 0)),
                  pl.BlockSpec((tq, qi.shape[1]), lambda b, i: (b * nq + i, 0)),
                  pl.BlockSpec((tq, wi.shape[1]), lambda b, i: (b * nq + i, 0)),
                  pl.BlockSpec((s, IDX_DIM), lambda b, i: (b, 0)),
                  pl.BlockSpec((s, 2 * LANES), lambda b, i: (b, 0))],
        out_specs=pl.BlockSpec((tq, H_A * HEAD_DIM), lambda b, i: (b * nq + i, 0)),
        out_shape=jax.ShapeDtypeStruct((nb * s, H_A * HEAD_DIM), F32),
        scratch_shapes=[pltpu.VMEM((s // LANES, tq, LANES), I32),
                        pltpu.VMEM((KV_A, grp * tq, LANES), BF16),
                        pltpu.VMEM((KV_A, grp * tq, LANES), F32),
                        pltpu.VMEM((KV_A, grp * tq, LANES), F32)],
        compiler_params=_cp("arbitrary", "arbitrary"), name="attn_a_prompt",
    )(qa, qi, wi, ki_bf, kva_bf)


def _lambda_value(lam_ref, lam_init):
    lp = lam_ref[...]
    a = jnp.sum(lp[0:1, :] * lp[1:2, :], axis=1, keepdims=True)
    b = jnp.sum(lp[2:3, :] * lp[3:4, :], axis=1, keepdims=True)
    return jnp.exp(a) - jnp.exp(b) + lam_init


def _attn_b_kernel(qb_ref, kvb_ref, lam_ref, sg_ref, o_ref, q_s, m_s, l_s, acc_s, *, lam_init, tq):
    i = pl.program_id(1)
    grp = H_B // KV_B
    lane = lax.broadcasted_iota(I32, (tq, LANES), 1)
    lo = lane < HEAD_DIM
    qb = qb_ref[...] * SCALE
    for k in range(KV_B):
        rows = []
        for g in range(grp):
            chunk = qb[:, (k * grp + g) * LANES:(k * grp + g + 1) * LANES]
            rows.append(jnp.where(lo, chunk, 0.0))
            rows.append(jnp.where(lo, 0.0, chunk))
        q_s[k] = jnp.concatenate(rows, axis=0).astype(BF16)
    _init_softmax(m_s, l_s, acc_s)

    def attend_chunk(c, bias):
        kv = kvb_ref[pl.ds(pl.multiple_of(c * TK, TK), TK), :]
        for k in range(KV_B):
            k_slab = kv[:, k * LANES:(k + 1) * LANES]
            v_slab = kv[:, (KV_B + k) * LANES:(KV_B + k + 1) * LANES]
            s = lax.dot_general(q_s[k], k_slab, _NT, preferred_element_type=F32)
            if bias is not None:
                s = (s.reshape(2 * grp, tq, TK) + bias[None]).reshape(2 * grp * tq, TK)
            _flash_step(s, v_slab, m_s, l_s, acc_s, k)

    n_full = (i * tq) // TK

    def full_chunk(c, carry):
        attend_chunk(c, None)
        return carry

    lax.fori_loop(0, n_full, full_chunk, 0)
    tpos = i * tq + lax.broadcasted_iota(I32, (tq, TK), 0)
    kpos = n_full * TK + lax.broadcasted_iota(I32, (tq, TK), 1)
    attend_chunk(n_full, jnp.where(kpos <= tpos, 0.0, NEG))

    lam = _lambda_value(lam_ref, lam_init)
    for k in range(KV_B):
        o = _flash_result(l_s, acc_s, k)
        for g in range(grp):
            d = o[(2 * g) * tq:(2 * g + 1) * tq] - lam * o[(2 * g + 1) * tq:(2 * g + 2) * tq]
            h = k * grp + g
            o_ref[:, h * LANES:(h + 1) * LANES] = _rmsnorm_rows(d, sg_ref[...]) * (1.0 - lam_init)


def _attn_b(qb, kvb_bf, lam_p, subln_g, lam_init, nb, s, tq):
    assert TK % tq == 0
    nq = s // tq
    grp = H_B // KV_B
    rows = 2 * grp * tq
    return pl.pallas_call(
        functools.partial(_attn_b_kernel, lam_init=lam_init, tq=tq),
        grid=(nb, nq),
        in_specs=[pl.BlockSpec((tq, qb.shape[1]), lambda b, i: (b * nq + i, 0)),
                  pl.BlockSpec((s, kvb_bf.shape[1]), lambda b, i: (b, 0)),
                  pl.BlockSpec(lam_p.shape, lambda b, i: (0, 0)),
                  pl.BlockSpec((1, 2 * HEAD_DIM), lambda b, i: (0, 0))],
        out_specs=pl.BlockSpec((tq, H_B * 2 * HEAD_DIM), lambda b, i: (b * nq + i, 0)),
        out_shape=jax.ShapeDtypeStruct((nb * s, H_B * 2 * HEAD_DIM), F32),
        scratch_shapes=[pltpu.VMEM((KV_B, rows, LANES), BF16),
                        pltpu.VMEM((KV_B, rows, LANES), F32),
                        pltpu.VMEM((KV_B, rows, LANES), F32),
                        pltpu.VMEM((KV_B, rows, LANES), F32)],
        compiler_params=_cp("arbitrary", "arbitrary"), name="attn_b_prompt",
    )(qb, kvb_bf, lam_p, subln_g.reshape(1, 2 * HEAD_DIM))


def _attn_c_kernel(q_ref, kv_ref, sink_ref, o_ref):
    i = pl.program_id(1)
    grp = H_C // KV_C
    base = jnp.maximum(i - 1, 0) * TQ
    kv = kv_ref[pl.ds(pl.multiple_of(base, TQ), 2 * TQ), :]
    k_slab, v_slab = kv[:, 0:LANES], kv[:, LANES:2 * LANES]
    qpos = i * TQ + lax.broadcasted_iota(I32, (TQ, 2 * TQ), 0)
    kpos = base + lax.broadcasted_iota(I32, (TQ, 2 * TQ), 1)
    bias = jnp.where((kpos >= qpos - WINDOW) & (kpos <= qpos), 0.0, NEG)
    q = q_ref[...] * SCALE
    pieces, slabs = [], []
    for k in range(KV_C):
        rows, sinks = [], []
        for g in range(grp):
            h = k * grp + g
            rows.append(_pad_head(q[:, (h // 2) * LANES:(h // 2 + 1) * LANES], h, k))
            sinks.append(jnp.broadcast_to(sink_ref[h:h + 1, :], (TQ, LANES)))
        qk = jnp.concatenate(rows, axis=0).astype(BF16)
        sink = jnp.concatenate(sinks, axis=0)
        s = lax.dot_general(qk, k_slab, _NT, preferred_element_type=F32)
        s = (s.reshape(grp, TQ, 2 * TQ) + bias[None]).reshape(grp * TQ, 2 * TQ)
        s0, s1 = s[:, 0:LANES], s[:, LANES:2 * LANES]
        m = jnp.maximum(jnp.max(jnp.maximum(s0, s1), axis=1, keepdims=True), sink)
        p = jnp.concatenate([jnp.exp(s0 - m), jnp.exp(s1 - m)], axis=1).astype(BF16)
        acc = jnp.dot(p, _with_ones(v_slab, k), preferred_element_type=F32)
        acc = acc + jnp.where(_own_half(acc.shape, k), 0.0, jnp.exp(sink - m))
        for g in range(grp):
            pieces.append(acc[g * TQ:(g + 1) * TQ])
            slabs.append(k)
    _gather_heads_norm(o_ref, pieces, slabs)


def _attn_c(q, kv_bf, sinks, nb, s):
    nq = s // TQ
    assert nq >= 2 and TQ == WINDOW
    return pl.pallas_call(
        _attn_c_kernel, grid=(nb, nq),
        in_specs=[pl.BlockSpec((TQ, q.shape[1]), lambda b, i: (b * nq + i, 0)),
                  pl.BlockSpec((s, 2 * LANES), lambda b, i: (b, 0)),
                  pl.BlockSpec((H_C, 1), lambda b, i: (0, 0))],
        out_specs=pl.BlockSpec((TQ, H_C * HEAD_DIM), lambda b, i: (b * nq + i, 0)),
        out_shape=jax.ShapeDtypeStruct((nb * s, H_C * HEAD_DIM), F32),
        compiler_params=_cp("arbitrary", "arbitrary"), name="attn_c_prompt",
    )(q, kv_bf, sinks.reshape(H_C, 1))


def _page_specs(rows, layer, pps):
    def make(u):
        return pl.BlockSpec((None, None, rows, LANES), lambda b, g, pt, *_: (layer, pt[b, g * pps + u], 0, 0))
    return [make(u) for u in range(pps)]


def _sa1_kernel(pt_ref, q_ref, w_ref, *refs, pps):
    pages, o_ref = refs[:pps], refs[pps]
    q, w = q_ref[...], w_ref[...]
    for u in range(pps):
        s = jnp.dot(q, pages[u][...].astype(BF16), preferred_element_type=F32)
        o_ref[:, u * LANES:(u + 1) * LANES] = jnp.sum(jnp.maximum(s, 0.0) * w, axis=0, keepdims=True)


def _sa1(page_table, q_blk, w_blk, idx_t, layer, pps):
    nseq, npages = page_table.shape
    gs = pltpu.PrefetchScalarGridSpec(
        num_scalar_prefetch=1, grid=(nseq, npages // pps),
        in_specs=[pl.BlockSpec((None, SUBLANES, IDX_DIM), lambda b, g, pt: (b, 0, 0)),
                  pl.BlockSpec((None, SUBLANES, LANES), lambda b, g, pt: (b, 0, 0))] + _page_specs(IDX_DIM, layer, pps),
        out_specs=pl.BlockSpec((None, 1, pps * LANES), lambda b, g, pt: (b, 0, g)))
    return pl.pallas_call(
        functools.partial(_sa1_kernel, pps=pps), grid_spec=gs,
        out_shape=jax.ShapeDtypeStruct((nseq, 1, npages * PAGE_SIZE), F32),
        compiler_params=_cp("arbitrary", "arbitrary"), name="index_scores_sample",
    )(page_table, q_blk, w_blk, *([idx_t] * pps))


def _sa2_kernel(i_ref, qi_ref, ki_ref, wi_ref, tau_ref, cut_ref, new_ref, key_s, *, n_sel, idx_bits):
    nseq, p = i_ref.shape
    nt = p // LANES
    for t in range(nt):
        key_s[t] = _to_key(i_ref[:, t * LANES:(t + 1) * LANES])
    qi = qi_ref[...].astype(BF16).astype(F32)
    ki = ki_ref[...].astype(BF16).astype(F32)
    inew = jnp.zeros((nseq, 1), F32)
    for h in range(IDX_HEADS):
        sh = jnp.sum(qi[:, h * IDX_DIM:(h + 1) * IDX_DIM] * ki, axis=1, keepdims=True)
        inew = inew + jnp.maximum(sh, 0.0) * wi_ref[:, h * LANES:h * LANES + 1]
    lane = lax.broadcasted_iota(I32, (nseq, LANES), 1)
    key_new = _to_key(jnp.broadcast_to(inew, (nseq, LANES)))
    key_s[nt] = jnp.where(lane == 0, key_new, PAD_KEY)
    for t in range(nt + 1, key_s.shape[0]):
        key_s[t] = jnp.full((nseq, LANES), PAD_KEY, I32)
    kk = jnp.full((nseq, 1), n_sel, F32)
    tau, cut = _select_threshold(key_s, key_s.shape[0] // CH, kk, idx_bits)
    tau_ref[...] = tau
    cut_ref[...] = cut
    kn = key_new[:, 0:1]
    new_ref[...] = ((kn > tau) | ((kn == tau) & (p <= cut))).astype(I32)


def _sa2(scores, qi, ki, wi):
    nseq, p = scores.shape
    ntiles = -(-(p // LANES + 1) // CH) * CH
    n_sel = min(TOPK_MAX, (p + 1) // 4)
    return pl.pallas_call(
        functools.partial(_sa2_kernel, n_sel=n_sel, idx_bits=p.bit_length()),
        out_shape=[jax.ShapeDtypeStruct((nseq, 1), I32)] * 3,
        scratch_shapes=[pltpu.VMEM((ntiles, nseq, LANES), I32)],
        compiler_params=pltpu.CompilerParams(vmem_limit_bytes=VMEM_LIMIT), name="topk_threshold_sample",
    )(scores, qi, ki, wi)


def _sa3_kernel(pt_ref, tau_ref, cut_ref, new_ref, q_ref, i_ref, kvn_ref, *refs, pps, ng):
    pages, o_ref, m_s, l_s, acc_s = refs[:pps], refs[pps], refs[pps + 1], refs[pps + 2], refs[pps + 3]
    b, g = pl.program_id(0), pl.program_id(1)

    @pl.when(g == 0)
    def _():
        _init_softmax(m_s, l_s, acc_s)

    tau, cut = tau_ref[b], cut_ref[b]
    q = q_ref[...]
    lane = lax.broadcasted_iota(I32, (1, LANES), 1)
    ss = []
    for u in range(pps):
        key = _to_key(i_ref[:, u * LANES:(u + 1) * LANES])
        sel = (key > tau) | ((key == tau) & ((g * pps + u) * PAGE_SIZE + lane <= cut))
        s = jnp.dot(q, pages[u][0:LANES, :].astype(BF16), preferred_element_type=F32) * SCALE
        ss.append(s + jnp.where(sel, 0.0, NEG))
    s = jnp.concatenate(ss, axis=1)
    m_old = m_s[0]
    m_new = jnp.maximum(m_old, jnp.max(s, axis=1, keepdims=True))
    alpha = jnp.exp(m_old - m_new)
    p = jnp.exp(s - m_new)
    l_s[0] = alpha * l_s[0] + jnp.sum(p, axis=1, keepdims=True)
    pv = jnp.zeros(acc_s.shape[1:], F32)
    for u in range(pps):
        pv = pv + lax.dot_general(p[:, u * LANES:(u + 1) * LANES].astype(BF16),
                                  pages[u][LANES:2 * LANES, :].astype(BF16), _NT, preferred_element_type=F32)
    acc_s[0] = alpha * acc_s[0] + pv
    m_s[0] = m_new

    @pl.when(g == ng - 1)
    def _():
        kvn = kvn_ref[...].astype(BF16).astype(F32)
        s_new = jnp.sum(q.astype(F32) * kvn[:, 0:LANES], axis=1, keepdims=True) * SCALE
        s_new = jnp.where(new_ref[b] > 0, s_new, NEG)
        m_fin = jnp.maximum(m_s[0], s_new)
        a_fin = jnp.exp(m_s[0] - m_fin)
        p_new = jnp.exp(s_new - m_fin)
        den = a_fin * l_s[0] + p_new
        o_ref[...] = (a_fin * acc_s[0] + p_new * kvn[:, LANES:2 * LANES]) / den


def _sa3(page_table, tau, cut, new, q_blk, scores, kv_new, kv_t, layer, pps):
    nseq, npages = page_table.shape
    ng = npages // pps
    nh = q_blk.shape[1]
    gs = pltpu.PrefetchScalarGridSpec(
        num_scalar_prefetch=4, grid=(nseq, ng),
        in_specs=[pl.BlockSpec((None, nh, LANES), lambda b, g, *_: (b, 0, 0)),
                  pl.BlockSpec((None, 1, pps * LANES), lambda b, g, *_: (b, 0, g)),
                  pl.BlockSpec((None, 1, 2 * LANES), lambda b, g, *_: (b, 0, 0))] + _page_specs(2 * LANES, layer, pps),
        out_specs=pl.BlockSpec((None, nh, LANES), lambda b, g, *_: (b, 0, 0)),
        scratch_shapes=[pltpu.VMEM((1, nh, 1), F32), pltpu.VMEM((1, nh, 1), F32), pltpu.VMEM((1, nh, LANES), F32)])
    return pl.pallas_call(
        functools.partial(_sa3_kernel, pps=pps, ng=ng), grid_spec=gs,
        out_shape=jax.ShapeDtypeStruct((nseq, nh, LANES), F32),
        compiler_params=_cp("arbitrary", "arbitrary"), name="attn_a_sample",
    )(page_table, tau, cut, new, q_blk, scores, kv_new, *([kv_t] * pps))


def _sb_kernel(pt_ref, q_ref, kvn_ref, lam_ref, sg_ref, *refs, pps, ng, lam_init):
    pages, o_ref, m_s, l_s, acc_s = refs[:pps], refs[pps], refs[pps + 1], refs[pps + 2], refs[pps + 3]
    g = pl.program_id(1)
    grp = H_B // KV_B
    stride = 2 * KV_B

    @pl.when(g == 0)
    def _():
        _init_softmax(m_s, l_s, acc_s)

    for k in range(KV_B):
        q = q_ref[k]
        ss = []
        for u in range(pps):
            k_rows = pages[u][pl.ds(k, PAGE_SIZE, stride=stride), :].astype(BF16)
            ss.append(lax.dot_general(q, k_rows, _NT, preferred_element_type=F32) * SCALE)
        s = jnp.concatenate(ss, axis=1)
        m_old = m_s[k]
        m_new = jnp.maximum(m_old, jnp.max(s, axis=1, keepdims=True))
        alpha = jnp.exp(m_old - m_new)
        p = jnp.exp(s - m_new)
        l_s[k] = alpha * l_s[k] + jnp.sum(p, axis=1, keepdims=True)
        pv = jnp.zeros(acc_s.shape[1:], F32)
        for u in range(pps):
            v_rows = pages[u][pl.ds(KV_B + k, PAGE_SIZE, stride=stride), :].astype(BF16)
            pv = pv + jnp.dot(p[:, u * LANES:(u + 1) * LANES].astype(BF16), v_rows, preferred_element_type=F32)
        acc_s[k] = alpha * acc_s[k] + pv
        m_s[k] = m_new

    @pl.when(g == ng - 1)
    def _():
        lam = _lambda_value(lam_ref, lam_init)
        kvn = kvn_ref[...].astype(BF16).astype(F32)
        for k in range(KV_B):
            q = q_ref[k].astype(F32)
            s_new = jnp.sum(q * kvn[:, k * LANES:(k + 1) * LANES], axis=1, keepdims=True) * SCALE
            m_fin = jnp.maximum(m_s[k], s_new)
            a_fin = jnp.exp(m_s[k] - m_fin)
            p_new = jnp.exp(s_new - m_fin)
            den = a_fin * l_s[k] + p_new
            o = (a_fin * acc_s[k] + p_new * kvn[:, (KV_B + k) * LANES:(KV_B + k + 1) * LANES]) / den
            for gg in range(grp):
                d = o[2 * gg:2 * gg + 1, :] - lam * o[2 * gg + 1:2 * gg + 2, :]
                h = k * grp + gg
                o_ref[h:h + 1, :] = _rmsnorm_rows(d, sg_ref[...]) * (1.0 - lam_init)


def _sb(page_table, q_blk, kv_new, lam_p, subln_g, kv_v, layer, pps, lam_init):
    nseq, npages = page_table.shape
    ng = npages // pps
    rows = q_blk.shape[2]
    gs = pltpu.PrefetchScalarGridSpec(
        num_scalar_prefetch=1, grid=(nseq, ng),
        in_specs=[pl.BlockSpec((None, KV_B, rows, LANES), lambda b, g, pt: (b, 0, 0, 0)),
                  pl.BlockSpec((None, 1, 2 * KV_B * LANES), lambda b, g, pt: (b, 0, 0)),
                  pl.BlockSpec(lam_p.shape, lambda b, g, pt: (0, 0)),
                  pl.BlockSpec((1, 2 * HEAD_DIM), lambda b, g, pt: (0, 0))]
                 + _page_specs(PAGE_SIZE * 2 * KV_B, layer, pps),
        out_specs=pl.BlockSpec((None, H_B, LANES), lambda b, g, pt: (b, 0, 0)),
        scratch_shapes=[pltpu.VMEM((KV_B, rows, 1), F32), pltpu.VMEM((KV_B, rows, 1), F32),
                        pltpu.VMEM((KV_B, rows, LANES), F32)])
    return pl.pallas_call(
        functools.partial(_sb_kernel, pps=pps, ng=ng, lam_init=lam_init), grid_spec=gs,
        out_shape=jax.ShapeDtypeStruct((nseq, H_B, LANES), F32),
        compiler_params=_cp("arbitrary", "arbitrary"), name="attn_b_sample",
    )(page_table, q_blk, kv_new, lam_p, subln_g.reshape(1, 2 * HEAD_DIM), *([kv_v] * pps))


def _sc_kernel(q_ref, kvn_ref, sink_ref, page_ref, o_ref):
    q = q_ref[...]
    kvn = kvn_ref[...].astype(BF16).astype(F32)
    sink = sink_ref[...]
    s = jnp.dot(q, page_ref[0:LANES, :].astype(BF16), preferred_element_type=F32) * SCALE
    s_new = jnp.sum(q.astype(F32) * kvn[:, 0:LANES], axis=1, keepdims=True) * SCALE
    m = jnp.maximum(jnp.maximum(jnp.max(s, axis=1, keepdims=True), s_new), sink)
    p = jnp.exp(s - m)
    p_new = jnp.exp(s_new - m)
    den = jnp.sum(p, axis=1, keepdims=True) + p_new + jnp.exp(sink - m)
    pv = lax.dot_general(p.astype(BF16), page_ref[LANES:2 * LANES, :].astype(BF16), _NT, preferred_element_type=F32)
    o_ref[...] = (pv + p_new * kvn[:, LANES:2 * LANES]) / den


def _sc(q_blk, kv_new, sinks, buf_t, layer):
    nseq = q_blk.shape[0]
    return pl.pallas_call(
        _sc_kernel, grid=(nseq,),
        in_specs=[pl.BlockSpec((None, H_C, LANES), lambda b: (b, 0, 0)),
                  pl.BlockSpec((None, 1, 2 * LANES), lambda b: (b, 0, 0)),
                  pl.BlockSpec((H_C, 1), lambda b: (0, 0)),
                  pl.BlockSpec((None, None, 2 * LANES, buf_t.shape[3]), lambda b: (layer, b, 0, 0))],
        out_specs=pl.BlockSpec((None, H_C, LANES), lambda b: (b, 0, 0)),
        out_shape=jax.ShapeDtypeStruct((nseq, H_C, LANES), F32),
        compiler_params=_cp("arbitrary"), name="attn_c_sample",
    )(q_blk, kv_new, sinks.reshape(H_C, 1), buf_t)


def _slab_pad(q, n_heads, grp):
    n = q.shape[0]
    qh = q.reshape(n, n_heads // grp, grp, HEAD_DIM)
    z = jnp.zeros_like(qh)
    lo = jnp.concatenate([qh[:, 0:1], z[:, 0:1]], axis=-1)
    hi = jnp.concatenate([z[:, 1:2], qh[:, 1:2]], axis=-1)
    return jnp.concatenate([lo, hi], axis=1).reshape(n, n_heads, 2 * HEAD_DIM)


def _slab_take(o, grp):
    n, n_heads, _ = o.shape
    oh = o.reshape(n, n_heads // grp, grp, 2, HEAD_DIM)
    return jnp.concatenate([oh[:, 0:1, :, 0], oh[:, 1:2, :, 1]], axis=1).reshape(n, n_heads * HEAD_DIM)


def kernel(x_prompt, x_sample, cache_a_kv, cache_a_idx, cache_b_kv, state_c_kv, state_ffn_conv, page_table, norm_mix_g, norm_ffn_g, norm_final_g, w_in_even, w_out_even, b_lambda, b_subln_g, w_in_odd, w_out_odd, c_sinks, ffn_w_up, ffn_conv_w, ffn_conv_b, ffn_w_down):
    nb, s, d = x_prompt.shape
    nseq = x_sample.shape[0]
    npages = page_table.shape[1]
    past = npages * PAGE_SIZE
    ff = ffn_w_down.shape[1]
    wb = state_c_kv.shape[2]
    assert s % TK == 0 and x_sample.shape[1] == 1 and wb <= WINDOW and wb == LANES
    assert KV_A == 2 and KV_B == 2 and KV_C == 2 and nseq % SUBLANES == 0

    tm_p = min(512, s)
    tm_f = min(512, s)
    tf_p = _pick_tile(ff, 1408)
    tf_s = _pick_tile(ff, 1408)
    tq_a = min(256, s)
    pps = min(64, npages)
    assert npages % pps == 0

    xp = x_prompt.reshape(nb * s, d)
    xs = x_sample.reshape(nseq, d)
    cos_p, sin_p = _rope_tables(jnp.arange(s))
    cos_s, sin_s = _rope_tables(jnp.full((nseq,), past))

    idx_t = jnp.transpose(cache_a_idx, (0, 1, 3, 2))
    akv_t = jnp.transpose(cache_a_kv, (0, 1, 3, 4, 5, 2)).reshape(cache_a_kv.shape[0], cache_a_kv.shape[1], 2 * LANES, PAGE_SIZE)
    bkv_v = cache_b_kv.reshape(cache_b_kv.shape[0], cache_b_kv.shape[1], PAGE_SIZE * 2 * KV_B, 2 * HEAD_DIM)
    ckv_t = jnp.transpose(state_c_kv, (0, 1, 3, 4, 5, 2)).reshape(state_c_kv.shape[0], nseq, 2 * LANES, wb)

    akv_p, akv_s, aidx_p, aidx_s, bkv_p, bkv_s = [], [], [], [], [], []
    ckv_p, ckv_s, conv_p, conv_s = [], [], [], []
    for l in range(DEPTH):
        j = l // 2
        if l % 2 == 0:
            lam_init = 0.8 - 0.6 * math.exp(-0.3 * l)
            w_in = _even_weights(w_in_even[j])
            w_out = w_out_even[j].astype(BF16)
            wo = [w_out[:H_A * HEAD_DIM], w_out[H_A * HEAD_DIM:]]
            qa, kva, qi, ki, wi, qb, kvb, kva_bf, ki_bf, kvb_bf = _proj(
                xp, norm_mix_g[l], w_in, cos_p, sin_p, _EVEN_PLAN, _EVEN_WIDTHS, tm_p)
            oa = _attn_a(qa, qi, wi, ki_bf, kva_bf, nb, s, tq_a)
            ob = _attn_b(qb, kvb_bf, b_lambda[j], b_subln_g[j], lam_init, nb, s, tq_a)
            xp = _outproj([oa, ob], xp, wo, tm_p)
            akv_p.append(kva.reshape(nb, s, 2, KV_A, HEAD_DIM))
            aidx_p.append(ki.reshape(nb, s, IDX_DIM))
            bkv_p.append(kvb.reshape(nb, s, 2, KV_B, 2 * HEAD_DIM))
            qa, kva, qi, ki, wi, qb, kvb = _proj(xs, norm_mix_g[l], w_in, cos_s, sin_s, _EVEN_PLAN, _EVEN_WIDTHS, nseq)[:7]
            zrow = SUBLANES - IDX_HEADS
            qi_blk = jnp.pad(qi.reshape(nseq, IDX_HEADS, IDX_DIM), ((0, 0), (0, zrow), (0, 0))).astype(BF16)
            wi_blk = jnp.pad(wi.reshape(nseq, IDX_HEADS, LANES), ((0, 0), (0, zrow), (0, 0)))
            scores = _sa1(page_table, qi_blk, wi_blk, idx_t, j, pps)
            tau, cut, new = _sa2(scores.reshape(nseq, past), qi, ki, wi)
            qa_blk = _slab_pad(qa, H_A, H_A // KV_A).astype(BF16)
            oa = _sa3(page_table, tau.reshape(nseq), cut.reshape(nseq), new.reshape(nseq), qa_blk, scores,
                      kva.reshape(nseq, 1, 2 * LANES), akv_t, j, pps)
            oa = _slab_take(oa, H_A // KV_A)
            qh = qb.reshape(nseq, KV_B, H_B // KV_B, 2, HEAD_DIM)
            z = jnp.zeros_like(qh[..., 0, :])
            qb_blk = jnp.stack([jnp.concatenate([qh[..., 0, :], z], -1), jnp.concatenate([z, qh[..., 1, :]], -1)], axis=3)
            qb_blk = qb_blk.reshape(nseq, KV_B, 2 * (H_B // KV_B), 2 * HEAD_DIM)
            qb_blk = jnp.pad(qb_blk, ((0, 0), (0, 0), (0, SUBLANES - qb_blk.shape[2]), (0, 0))).astype(BF16)
            ob = _sb(page_table, qb_blk, kvb.reshape(nseq, 1, 2 * KV_B * LANES), b_lambda[j], b_subln_g[j], bkv_v, j, pps, lam_init)
            xs = _outproj([oa, ob.reshape(nseq, H_B * 2 * HEAD_DIM)], xs, wo, nseq)
            akv_s.append(kva.reshape(nseq, 1, 2, KV_A, HEAD_DIM))
            aidx_s.append(ki.reshape(nseq, 1, IDX_DIM))
            bkv_s.append(kvb.reshape(nseq, 1, 2, KV_B, 2 * HEAD_DIM))
        else:
            w_in = w_in_odd[j].astype(BF16)
            w_out = w_out_odd[j].astype(BF16)
            q, kv, kv_bf = _proj(xp, norm_mix_g[l], w_in, cos_p, sin_p, _ODD_PLAN, _ODD_WIDTHS, tm_p)
            o = _attn_c(q, kv_bf, c_sinks[j], nb, s)
            xp = _outproj([o], xp, [w_out], tm_p)
            ckv_p.append(kv.reshape(nb, s, 2, KV_C, HEAD_DIM)[:, s - min(WINDOW, s):])
            q, kv = _proj(xs, norm_mix_g[l], w_in, cos_s, sin_s, _ODD_PLAN, _ODD_WIDTHS, nseq)[:2]
            q_blk = _slab_pad(q, H_C, H_C // KV_C).astype(BF16)
            o = _sc(q_blk, kv.reshape(nseq, 1, 2 * LANES), c_sinks[j], ckv_t, j)
            xs = _outproj([_slab_take(o, H_C // KV_C)], xs, [w_out], nseq)
            ckv_s.append(jnp.concatenate([state_c_kv[j][:, 1:], kv.reshape(nseq, 1, 2, KV_C, HEAD_DIM)], axis=1))
        w_up = ffn_w_up[l].astype(BF16)
        wg, wv, wd = w_up[:, :ff], w_up[:, ff:], ffn_w_down[l].astype(BF16)
        xp, tail = _ffn(xp, norm_ffn_g[l], wg, wv, ffn_conv_w[l], ffn_conv_b[l], wd, tm_f, tf_p, rows_per_seq=s)
        conv_p.append(tail.reshape(nb, s // tm_f, SUBLANES, ff)[:, -1, SUBLANES - (CONV_W - 1):, :])
        hist = state_ffn_conv[l]
        xs, g_new = _ffn(xs, norm_ffn_g[l], wg, wv, ffn_conv_w[l], ffn_conv_b[l], wd, nseq, tf_s, hist=hist)
        conv_s.append(jnp.concatenate([hist[:, 1:], g_new[:, None, :]], axis=1))
    y_prompt = _final_norm(xp, norm_final_g, tm_p).reshape(nb, s, d)
    y_sample = _final_norm(xs, norm_final_g, nseq).reshape(nseq, 1, d)
    return (y_prompt, y_sample,
            jnp.stack(akv_p), jnp.stack(akv_s), jnp.stack(aidx_p), jnp.stack(aidx_s),
            jnp.stack(bkv_p), jnp.stack(bkv_s), jnp.stack(ckv_p), jnp.stack(ckv_s),
            jnp.stack(conv_p), jnp.stack(conv_s))
```

```python
import functools
import math

import jax
import jax.numpy as jnp
from jax import lax
from jax.experimental import pallas as pl
from jax.experimental.pallas import tpu as pltpu

F32, BF16, I32 = jnp.float32, jnp.bfloat16, jnp.int32

HEAD_DIM = 64
ROPE_THETA = 10000.0
EPS = 1e-6
H_A, KV_A, IDX_HEADS, IDX_DIM, TOPK_MAX = 8, 2, 4, 64, 256
H_B, KV_B = 4, 2
H_C, KV_C, WINDOW = 16, 2, 128
CONV_W = 3
PAGE_SIZE = 128
DEPTH = 4

LANES = 128
SUBLANES = 8
VMEM_LIMIT = 56 * 1024 * 1024

TQ = 128
CH = 4
TK = CH * TQ
NEG = -1e30
PAD_KEY = -2139095041
SCALE = HEAD_DIM ** -0.5

_NT = (((1,), (1,)), ((), ()))


def _cp(*sem):
    return pltpu.CompilerParams(dimension_semantics=sem, vmem_limit_bytes=VMEM_LIMIT)


def _pick_tile(n, target):
    if n <= target:
        return n
    best = None
    for t in range(LANES, target + 1, LANES):
        if n % t == 0:
            best = t
    assert best is not None, (n, target)
    return best


def _rmsnorm_rows(x, g):
    return x * lax.rsqrt(jnp.mean(x * x, axis=-1, keepdims=True) + EPS) * g


def _to_key(x):
    b = lax.bitcast_convert_type(x, I32)
    return jnp.where(b < 0, b ^ jnp.int32(0x7FFFFFFF), b)


def _rope_tables(pos):
    half = HEAD_DIM // 2
    inv = ROPE_THETA ** (-jnp.arange(half, dtype=F32) / half)
    ang = pos.astype(F32)[:, None] * inv[None, :]
    c, s = jnp.cos(ang), jnp.sin(ang)
    return jnp.tile(c, (1, 4)), jnp.concatenate([-s, s, -s, s], axis=1)


def _proj_kernel(x_ref, g_ref, w_ref, cos_ref, sin_ref, *out_refs, plan):
    h = _rmsnorm_rows(x_ref[...], g_ref[...]).astype(BF16)
    cos, sin = cos_ref[...], sin_ref[...]
    lane = lax.broadcasted_iota(I32, cos.shape, 1)
    first_half = (lane % HEAD_DIM) < HEAD_DIM // 2
    for col0, width, rope_w, store_w, o_idxs in plan:
        y = jnp.dot(h, w_ref[:, col0:col0 + width], preferred_element_type=F32)
        for c in range(width // LANES):
            yc = y[:, c * LANES:(c + 1) * LANES]
            if c * LANES < rope_w:
                partner = jnp.where(first_half, pltpu.roll(yc, LANES - HEAD_DIM // 2, 1),
                                    pltpu.roll(yc, HEAD_DIM // 2, 1))
                yc = yc * cos + partner * sin
            ow = min(LANES, store_w - c * LANES)
            for o_idx in o_idxs:
                out_refs[o_idx][:, c * LANES:c * LANES + ow] = yc[:, :ow].astype(out_refs[o_idx].dtype)


def _proj(x, g, w, cos, sin, plan, out_widths, tm):
    m, d = x.shape
    n_pos = cos.shape[0] // tm
    return pl.pallas_call(
        functools.partial(_proj_kernel, plan=plan),
        grid=(m // tm,),
        in_specs=[pl.BlockSpec((tm, d), lambda i: (i, 0)),
                  pl.BlockSpec((1, d), lambda i: (0, 0)),
                  pl.BlockSpec(w.shape, lambda i: (0, 0)),
                  pl.BlockSpec((tm, LANES), lambda i: (i % n_pos, 0)),
                  pl.BlockSpec((tm, LANES), lambda i: (i % n_pos, 0))],
        out_specs=[pl.BlockSpec((tm, ow), lambda i: (i, 0)) for ow, _ in out_widths],
        out_shape=[jax.ShapeDtypeStruct((m, ow), dt) for ow, dt in out_widths],
        compiler_params=_cp("arbitrary"), name="norm_proj_rope",
    )(x, g.reshape(1, d), w, cos, sin)


def _even_weights(w):
    sizes = (H_A * HEAD_DIM, KV_A * HEAD_DIM, KV_A * HEAD_DIM, IDX_HEADS * IDX_DIM, IDX_DIM, IDX_HEADS,
             H_B * 2 * HEAD_DIM, KV_B * 2 * HEAD_DIM, KV_B * 2 * HEAD_DIM)
    offs = [0]
    for s in sizes:
        offs.append(offs[-1] + s)
    qa, ka, va, qi, ki, wi, qb, kb, vb = (w[:, offs[i]:offs[i + 1]] for i in range(9))
    ki = jnp.pad(ki, ((0, 0), (0, LANES - IDX_DIM)))
    wi = jnp.repeat(wi * (IDX_HEADS ** -0.5 * IDX_DIM ** -0.5), LANES, axis=1)
    return jnp.concatenate([qa, ka, va, qi, ki, wi, qb, kb, vb], axis=1).astype(BF16)


_EVEN_WIDTHS = ((512, F32), (256, F32), (256, F32), (64, F32), (512, F32), (512, F32), (512, F32),
                (256, BF16), (64, BF16), (512, BF16))
_EVEN_PLAN = ((0, 512, 512, 512, (0,)), (512, 256, 128, 256, (1, 7)), (768, 256, 256, 256, (2,)),
              (1024, 128, 128, 64, (3, 8)), (1152, 512, 0, 512, (4,)), (1664, 512, 512, 512, (5,)),
              (2176, 512, 256, 512, (6, 9)))
_ODD_WIDTHS = ((1024, F32), (256, F32), (256, BF16))
_ODD_PLAN = ((0, 1024, 1024, 1024, (0,)), (1024, 256, 128, 256, (1, 2)))


def _outproj_kernel(*refs, n_in):
    o_refs, x_ref, w_refs, y_ref = refs[:n_in], refs[n_in], refs[n_in + 1:2 * n_in + 1], refs[2 * n_in + 1]
    acc = x_ref[...]
    for o_ref, w_ref in zip(o_refs, w_refs):
        acc = acc + jnp.dot(o_ref[...].astype(BF16), w_ref[...], preferred_element_type=F32)
    y_ref[...] = acc


def _outproj(os_, x, ws, tm):
    m, d = x.shape
    n = len(os_)
    return pl.pallas_call(
        functools.partial(_outproj_kernel, n_in=n),
        grid=(m // tm,),
        in_specs=([pl.BlockSpec((tm, o.shape[1]), lambda i: (i, 0)) for o in os_]
                  + [pl.BlockSpec((tm, d), lambda i: (i, 0))]
                  + [pl.BlockSpec(w.shape, lambda i: (0, 0)) for w in ws]),
        out_specs=pl.BlockSpec((tm, d), lambda i: (i, 0)),
        out_shape=jax.ShapeDtypeStruct((m, d), F32),
        compiler_params=_cp("arbitrary"), name="out_proj_residual",
    )(*os_, x, *ws)


def _final_norm_kernel(x_ref, g_ref, y_ref):
    y_ref[...] = _rmsnorm_rows(x_ref[...], g_ref[...])


def _final_norm(x, g, tm):
    m, d = x.shape
    return pl.pallas_call(
        _final_norm_kernel, grid=(m // tm,),
        in_specs=[pl.BlockSpec((tm, d), lambda i: (i, 0)), pl.BlockSpec((1, d), lambda i: (0, 0))],
        out_specs=pl.BlockSpec((tm, d), lambda i: (i, 0)),
        out_shape=jax.ShapeDtypeStruct((m, d), F32),
        compiler_params=_cp("arbitrary"), name="final_norm",
    )(x, g.reshape(1, d))


def _ffn_kernel(*refs, seq_mode, tm, rows_per_seq, nf):
    if seq_mode:
        x_ref, gn_ref, wg_ref, wv_ref, cw_ref, cb_ref, wd_ref, y_ref, tail_ref, h_s, acc_s, gs_s, carry_s = refs
    else:
        (x_ref, gn_ref, wg_ref, wv_ref, cw_ref, cb_ref, wd_ref, gm2_ref, gm1_ref,
         y_ref, gout_ref, h_s, acc_s) = refs
    m, f = pl.program_id(0), pl.program_id(1)

    @pl.when(f == 0)
    def _():
        h_s[...] = _rmsnorm_rows(x_ref[...], gn_ref[...]).astype(BF16)
        acc_s[...] = jnp.zeros_like(acc_s)

    h = h_s[...]
    g = jnp.dot(h, wg_ref[...], preferred_element_type=F32)
    v = jnp.dot(h, wv_ref[...], preferred_element_type=F32)
    if seq_mode:
        starts_seq = (m * tm) % rows_per_seq == 0

        @pl.when(starts_seq)
        def _():
            gs_s[0:SUBLANES, :] = jnp.zeros((SUBLANES, gs_s.shape[1]), F32)

        @pl.when(jnp.logical_not(starts_seq))
        def _():
            gs_s[0:SUBLANES, :] = carry_s[f]

        gs_s[SUBLANES:SUBLANES + tm, :] = g
        last = g[tm - SUBLANES:tm, :]
        carry_s[f] = last
        tail_ref[0] = last
        gm1 = gs_s[SUBLANES - 1:SUBLANES - 1 + tm, :]
        gm2 = gs_s[SUBLANES - 2:SUBLANES - 2 + tm, :]
    else:
        gm1, gm2 = gm1_ref[...], gm2_ref[...]
        gout_ref[...] = g
    cw = cw_ref[...]
    gc = cw[0:1, :] * gm2 + cw[1:2, :] * gm1 + cw[2:3, :] * g + cb_ref[...]
    act = 0.5 * gc * (1.0 + lax.erf(gc * math.sqrt(0.5))) * v
    acc_s[...] += jnp.dot(act.astype(BF16), wd_ref[...], preferred_element_type=F32)

    @pl.when(f == nf - 1)
    def _():
        y_ref[...] = x_ref[...] + acc_s[...]


def _ffn(x, gn, wg, wv, cw, cb, wd, tm, tf, rows_per_seq=None, hist=None):
    m, d = x.shape
    ff = wg.shape[1]
    nf = ff // tf
    seq_mode = hist is None
    common = [pl.BlockSpec((tm, d), lambda i, f: (i, 0)),
              pl.BlockSpec((1, d), lambda i, f: (0, 0)),
              pl.BlockSpec((d, tf), lambda i, f: (0, f)),
              pl.BlockSpec((d, tf), lambda i, f: (0, f)),
              pl.BlockSpec((CONV_W, tf), lambda i, f: (0, f)),
              pl.BlockSpec((1, tf), lambda i, f: (0, f)),
              pl.BlockSpec((tf, d), lambda i, f: (f, 0))]
    y_spec = pl.BlockSpec((tm, d), lambda i, f: (i, 0))
    y_shape = jax.ShapeDtypeStruct((m, d), F32)
    scratch = [pltpu.VMEM((tm, d), BF16), pltpu.VMEM((tm, d), F32)]
    args = [x, gn.reshape(1, d), wg, wv, cw, cb.reshape(1, ff), wd]
    if seq_mode:
        assert rows_per_seq % tm == 0 and tm >= SUBLANES
        in_specs = common
        out_specs = [y_spec, pl.BlockSpec((1, SUBLANES, tf), lambda i, f: (i, 0, f))]
        out_shape = [y_shape, jax.ShapeDtypeStruct((m // tm, SUBLANES, ff), F32)]
        scratch += [pltpu.VMEM((tm + SUBLANES, tf), F32), pltpu.VMEM((nf, SUBLANES, tf), F32)]
    else:
        in_specs = common + [pl.BlockSpec((tm, tf), lambda i, f: (i, f)), pl.BlockSpec((tm, tf), lambda i, f: (i, f))]
        out_specs = [y_spec, pl.BlockSpec((tm, tf), lambda i, f: (i, f))]
        out_shape = [y_shape, jax.ShapeDtypeStruct((m, ff), F32)]
        args += [hist[:, 0, :], hist[:, 1, :]]
    return pl.pallas_call(
        functools.partial(_ffn_kernel, seq_mode=seq_mode, tm=tm, rows_per_seq=rows_per_seq, nf=nf),
        grid=(m // tm, nf), in_specs=in_specs, out_specs=out_specs, out_shape=out_shape,
        scratch_shapes=scratch, compiler_params=_cp("arbitrary", "arbitrary"),
        name="conv_glu_seq" if seq_mode else "conv_glu_tok",
    )(*args)


def _count(key_s, nchunks, pred):
    return jnp.sum(_count_lanes(key_s, nchunks, pred), axis=1, keepdims=True)


def _count_lanes(key_s, nchunks, pred):
    rows = key_s.shape[1]
    per_iter = min(CH, max(1, CH * TQ // rows))
    assert CH % per_iter == 0

    def body(c, acc):
        for u in range(per_iter):
            t = c * per_iter + u
            acc = acc + jnp.where(pred(key_s[t], t), 1.0, 0.0)
        return acc

    return lax.fori_loop(0, nchunks * (CH // per_iter), body, jnp.zeros((rows, LANES), F32))


def _select_threshold_groups(views, nchunks, kks):
    shape = (views[0].shape[1], LANES)

    def lane_sum(a):
        return jnp.sum(a, axis=1, keepdims=True)

    accs = [_count_lanes(v, nc, lambda k, t: k >= 0) for v, nc in zip(views, nchunks)]
    taus = tuple(jnp.where(lane_sum(a) >= kk, jnp.int32(0), jnp.int32(-2 ** 31)) for a, kk in zip(accs, kks))

    def value_bit(i, taus):
        bit = lax.shift_left(jnp.int32(1), jnp.asarray(30 - i, I32))
        cands = [tau + bit for tau in taus]
        accs = []
        for v, nc, cand in zip(views, nchunks, cands):
            cand_b = jnp.broadcast_to(cand, shape)
            accs.append(_count_lanes(v, nc, lambda k, t, cand_b=cand_b: k >= cand_b))
        return tuple(jnp.where(lane_sum(a) >= kk, cand, tau) for a, kk, cand, tau in zip(accs, kks, cands, taus))

    taus = lax.fori_loop(0, 31, value_bit, taus)
    accs = []
    for v, nc, tau in zip(views, nchunks, taus):
        tau_b = jnp.broadcast_to(tau, shape)
        accs.append(_count_lanes(v, nc, lambda k, t, tau_b=tau_b: k > tau_b))
    return taus, [kk - lane_sum(a) for a, kk in zip(accs, kks)]


def _select_threshold(key_s, nchunks, kk, idx_bits):
    rows = key_s.shape[1]
    shape = (rows, LANES)
    lane = lax.broadcasted_iota(I32, shape, 1)
    c0 = _count(key_s, nchunks, lambda k, t: k >= 0)
    tau = jnp.where(c0 >= kk, jnp.int32(0), jnp.int32(-2 ** 31))

    def value_bit(i, tau):
        cand = tau + lax.shift_left(jnp.int32(1), jnp.asarray(30 - i, I32))
        cand_b = jnp.broadcast_to(cand, shape)
        c = _count(key_s, nchunks, lambda k, t: k >= cand_b)
        return jnp.where(c >= kk, cand, tau)

    tau = lax.fori_loop(0, 31, value_bit, tau)
    tau_b = jnp.broadcast_to(tau, shape)
    need = kk - _count(key_s, nchunks, lambda k, t: k > tau_b)
    if idx_bits is None:
        return tau, need

    def index_bit(i, cut):
        cand = cut + lax.shift_left(jnp.int32(1), jnp.asarray(idx_bits - 1 - i, I32))
        cand_b = jnp.broadcast_to(cand, shape)
        c = _count(key_s, nchunks, lambda k, t: (k == tau_b) & (lane < cand_b - t * LANES))
        return jnp.where(c < need, cand, cut)

    cut = lax.fori_loop(0, idx_bits, index_bit, jnp.zeros((rows, 1), I32))
    return tau, cut


def _flash_step(s, v, m_ref, l_ref, acc_ref, k):
    nt = s.shape[1] // LANES
    tiles = [s[:, t * LANES:(t + 1) * LANES] for t in range(nt)]
    smax = tiles[0]
    for t in range(1, nt):
        smax = jnp.maximum(smax, tiles[t])
    m_old = m_ref[k]
    m_new = jnp.maximum(m_old, jnp.max(smax, axis=1, keepdims=True))
    alpha = jnp.exp(m_old - m_new)
    ps = [jnp.exp(t_ - m_new) for t_ in tiles]
    if l_ref is not None:
        lsum = ps[0]
        for t in range(1, nt):
            lsum = lsum + ps[t]
        l_ref[k] = alpha * l_ref[k] + lsum
    p = jnp.concatenate(ps, axis=1).astype(BF16)
    acc_ref[k] = alpha * acc_ref[k] + jnp.dot(p, v, preferred_element_type=F32)
    m_ref[k] = m_new


def _flash_result(l_ref, acc_ref, k):
    return acc_ref[k] / jnp.sum(l_ref[k], axis=1, keepdims=True)


def _pad_head(chunk, head, slab):
    lane = lax.broadcasted_iota(I32, chunk.shape, 1)
    if head % 2 != slab:
        chunk = pltpu.roll(chunk, HEAD_DIM, 1)
    keep = (lane < HEAD_DIM) if slab == 0 else (lane >= HEAD_DIM)
    return jnp.where(keep, chunk, 0.0)


def _own_half(shape, slab):
    lane = lax.broadcasted_iota(I32, shape, 1)
    return (lane < HEAD_DIM) if slab == 0 else (lane >= HEAD_DIM)


def _with_ones(v_slab, slab):
    return jnp.where(_own_half(v_slab.shape, slab), v_slab, jnp.ones_like(v_slab))


def _gather_heads_norm(o_ref, pieces, slabs):
    lo = _own_half(pieces[0].shape, 0)
    for j in range(len(pieces) // 2):
        a, b = pieces[2 * j], pieces[2 * j + 1]
        ra, rb = pltpu.roll(a, HEAD_DIM, 1), pltpu.roll(b, HEAD_DIM, 1)
        a = a / ra if slabs[2 * j] == 0 else ra / a
        b = b / rb if slabs[2 * j + 1] == 1 else rb / b
        o_ref[:, j * LANES:(j + 1) * LANES] = jnp.where(lo, a, b)


def _init_softmax(m_ref, l_ref, acc_ref):
    m_ref[...] = jnp.full(m_ref.shape, NEG, F32)
    if l_ref is not None:
        l_ref[...] = jnp.zeros_like(l_ref)
    acc_ref[...] = jnp.zeros_like(acc_ref)


def _attn_a_kernel(qa_ref, qi_ref, wi_ref, ki_ref, kva_ref, o_ref, key_s, q_s, m_s, acc_s, *, n_sel, tq):
    i = pl.program_id(1)
    nch = ((i + 1) * tq + TK - 1) // TK
    grp = H_A // KV_A
    tpos_w = i * tq + lax.broadcasted_iota(I32, (tq, TK), 0)
    lane_w = lax.broadcasted_iota(I32, (tq, TK), 1)

    qi = qi_ref[...].astype(BF16)
    qs = jnp.concatenate([qi[:, h * IDX_DIM:(h + 1) * IDX_DIM] for h in range(IDX_HEADS)], axis=0)
    ws = jnp.concatenate([wi_ref[:, h * LANES:(h + 1) * LANES] for h in range(IDX_HEADS)], axis=0)
    ws = jnp.concatenate([ws] * CH, axis=1)

    def score_chunk(c, carry):
        k_rows = ki_ref[pl.ds(pl.multiple_of(c * TK, TK), TK), :]
        s = lax.dot_general(qs, k_rows, _NT, preferred_element_type=F32)
        x = jnp.maximum(s, 0.0) * ws
        sc = x[0:tq]
        for h in range(1, IDX_HEADS):
            sc = sc + x[h * tq:(h + 1) * tq]
        key = _to_key(jnp.where(c * TK + lane_w <= tpos_w, sc, -jnp.inf))
        for u in range(CH):
            key_s[c * CH + u] = key[:, u * LANES:(u + 1) * LANES]
        return carry

    lax.fori_loop(0, nch, score_chunk, 0)

    views, nchs, kks = [], [], []
    for r in range(tq // TQ):
        row1 = lax.broadcasted_iota(I32, (TQ, 1), 0)
        kks.append(jnp.minimum(n_sel, i * tq + r * TQ + row1 + 1).astype(F32))
        nchs.append((i * tq + (r + 1) * TQ + TK - 1) // TK)
        views.append(key_s.at[:, pl.ds(r * TQ, TQ), :])
    taus, needs = _select_threshold_groups(views, nchs, kks)
    tau_b = jnp.concatenate([jnp.broadcast_to(t, (TQ, LANES)) for t in taus], axis=0)
    need_b = jnp.concatenate([jnp.broadcast_to(n, (TQ, LANES)) for n in needs], axis=0)
    r_i = lax.broadcasted_iota(I32, (LANES, LANES), 0)
    c_i = lax.broadcasted_iota(I32, (LANES, LANES), 1)
    prefix_m = jnp.where(r_i <= c_i, 1.0, 0.0).astype(BF16)
    total_m = jnp.ones((LANES, LANES), BF16)

    qa = qa_ref[...]
    for k in range(KV_A):
        rows = []
        for g in range(grp):
            h = k * grp + g
            rows.append(_pad_head(qa[:, (h // 2) * LANES:(h // 2 + 1) * LANES], h, k))
        q_s[k] = (jnp.concatenate(rows, axis=0) * SCALE).astype(BF16)
    _init_softmax(m_s, None, acc_s)

    def attend_chunk(c, ties_before):
        kv = kva_ref[pl.ds(pl.multiple_of(c * TK, TK), TK), :]
        k_slab, v_slab = kv[:, 0:LANES], kv[:, LANES:2 * LANES]
        biases = []
        for u in range(CH):
            key = key_s[c * CH + u]
            tie = key == tau_b
            tie_bf = jnp.where(tie, 1.0, 0.0).astype(BF16)
            rank = ties_before + jnp.dot(tie_bf, prefix_m, preferred_element_type=F32)
            sel = (key > tau_b) | (tie & (rank <= need_b))
            ties_before = ties_before + jnp.dot(tie_bf, total_m, preferred_element_type=F32)
            biases.append(jnp.where(sel, 0.0, NEG))
        bias = jnp.concatenate(biases, axis=1)
        for k in range(KV_A):
            s = lax.dot_general(q_s[k], k_slab, _NT, preferred_element_type=F32)
            s = (s.reshape(grp, tq, TK) + bias[None]).reshape(grp * tq, TK)
            _flash_step(s, _with_ones(v_slab, k), m_s, None, acc_s, k)
        return ties_before

    lax.fori_loop(0, nch, attend_chunk, jnp.zeros((tq, LANES), F32))

    pieces, slabs = [], []
    for k in range(KV_A):
        acc = acc_s[k]
        for g in range(grp):
            pieces.append(acc[g * tq:(g + 1) * tq])
            slabs.append(k)
    _gather_heads_norm(o_ref, pieces, slabs)


def _attn_a(qa, qi, wi, ki_bf, kva_bf, nb, s, tq):
    nq = s // tq
    n_sel = min(TOPK_MAX, s // 4)
    grp = H_A // KV_A
    return pl.pallas_call(
        functools.partial(_attn_a_kernel, n_sel=n_sel, tq=tq),
        grid=(nb, nq),
        in_specs=[pl.BlockSpec((tq, qa.shape[1]), lambda b, i: (b * nq + i, 0)),
                  pl.BlockSpec((tq, qi.shape[1]), lambda b, i: (b * nq + i, 0)),
                  pl.BlockSpec((tq, wi.shape[1]), lambda b, i: (b * nq + i, 0)),
                  pl.BlockSpec((s, IDX_DIM), lambda b, i: (b, 0)),
                  pl.BlockSpec((s, 2 * LANES), lambda b, i: (b, 0))],
        out_specs=pl.BlockSpec((tq, H_A * HEAD_DIM), lambda b, i: (b * nq + i, 0)),
        out_shape=jax.ShapeDtypeStruct((nb * s, H_A * HEAD_DIM), F32),
        scratch_shapes=[pltpu.VMEM((s // LANES, tq, LANES), I32),
                        pltpu.VMEM((KV_A, grp * tq, LANES), BF16),
                        pltpu.VMEM((KV_A, grp * tq, LANES), F32),
                        pltpu.VMEM((KV_A, grp * tq, LANES), F32)],
        compiler_params=_cp("arbitrary", "arbitrary"), name="attn_a_prompt",
    )(qa, qi, wi, ki_bf, kva_bf)


def _lambda_value(lam_ref, lam_init):
    lp = lam_ref[...]
    a = jnp.sum(lp[0:1, :] * lp[1:2, :], axis=1, keepdims=True)
    b = jnp.sum(lp[2:3, :] * lp[3:4, :], axis=1, keepdims=True)
    return jnp.exp(a) - jnp.exp(b) + lam_init


def _attn_b_kernel(qb_ref, kvb_ref, lam_ref, sg_ref, o_ref, q_s, m_s, l_s, acc_s, *, lam_init, tq):
    i = pl.program_id(1)
    grp = H_B // KV_B
    lane = lax.broadcasted_iota(I32, (tq, LANES), 1)
    lo = lane < HEAD_DIM
    qb = qb_ref[...] * SCALE
    for k in range(KV_B):
        rows = []
        for g in range(grp):
            chunk = qb[:, (k * grp + g) * LANES:(k * grp + g + 1) * LANES]
            rows.append(jnp.where(lo, chunk, 0.0))
            rows.append(jnp.where(lo, 0.0, chunk))
        q_s[k] = jnp.concatenate(rows, axis=0).astype(BF16)
    _init_softmax(m_s, l_s, acc_s)

    def attend_chunk(c, bias):
        kv = kvb_ref[pl.ds(pl.multiple_of(c * TK, TK), TK), :]
        for k in range(KV_B):
            k_slab = kv[:, k * LANES:(k + 1) * LANES]
            v_slab = kv[:, (KV_B + k) * LANES:(KV_B + k + 1) * LANES]
            s = lax.dot_general(q_s[k], k_slab, _NT, preferred_element_type=F32)
            if bias is not None:
                s = (s.reshape(2 * grp, tq, TK) + bias[None]).reshape(2 * grp * tq, TK)
            _flash_step(s, v_slab, m_s, l_s, acc_s, k)

    n_full = (i * tq) // TK

    def full_chunk(c, carry):
        attend_chunk(c, None)
        return carry

    lax.fori_loop(0, n_full, full_chunk, 0)
    tpos = i * tq + lax.broadcasted_iota(I32, (tq, TK), 0)
    kpos = n_full * TK + lax.broadcasted_iota(I32, (tq, TK), 1)
    attend_chunk(n_full, jnp.where(kpos <= tpos, 0.0, NEG))

    lam = _lambda_value(lam_ref, lam_init)
    for k in range(KV_B):
        o = _flash_result(l_s, acc_s, k)
        for g in range(grp):
            d = o[(2 * g) * tq:(2 * g + 1) * tq] - lam * o[(2 * g + 1) * tq:(2 * g + 2) * tq]
            h = k * grp + g
            o_ref[:, h * LANES:(h + 1) * LANES] = _rmsnorm_rows(d, sg_ref[...]) * (1.0 - lam_init)


def _attn_b(qb, kvb_bf, lam_p, subln_g, lam_init, nb, s, tq):
    assert TK % tq == 0
    nq = s // tq
    grp = H_B // KV_B
    rows = 2 * grp * tq
    return pl.pallas_call(
        functools.partial(_attn_b_kernel, lam_init=lam_init, tq=tq),
        grid=(nb, nq),
        in_specs=[pl.BlockSpec((tq, qb.shape[1]), lambda b, i: (b * nq + i, 0)),
                  pl.BlockSpec((s, kvb_bf.shape[1]), lambda b, i: (b, 0)),
                  pl.BlockSpec(lam_p.shape, lambda b, i: (0, 0)),
                  pl.BlockSpec((1, 2 * HEAD_DIM), lambda b, i: (0, 0))],
        out_specs=pl.BlockSpec((tq, H_B * 2 * HEAD_DIM), lambda b, i: (b * nq + i, 0)),
        out_shape=jax.ShapeDtypeStruct((nb * s, H_B * 2 * HEAD_DIM), F32),
        scratch_shapes=[pltpu.VMEM((KV_B, rows, LANES), BF16),
                        pltpu.VMEM((KV_B, rows, LANES), F32),
                        pltpu.VMEM((KV_B, rows, LANES), F32),
                        pltpu.VMEM((KV_B, rows, LANES), F32)],
        compiler_params=_cp("arbitrary", "arbitrary"), name="attn_b_prompt",
    )(qb, kvb_bf, lam_p, subln_g.reshape(1, 2 * HEAD_DIM))


def _attn_c_kernel(q_ref, kv_ref, sink_ref, o_ref):
    i = pl.program_id(1)
    grp = H_C // KV_C
    base = jnp.maximum(i - 1, 0) * TQ
    kv = kv_ref[pl.ds(pl.multiple_of(base, TQ), 2 * TQ), :]
    k_slab, v_slab = kv[:, 0:LANES], kv[:, LANES:2 * LANES]
    qpos = i * TQ + lax.broadcasted_iota(I32, (TQ, 2 * TQ), 0)
    kpos = base + lax.broadcasted_iota(I32, (TQ, 2 * TQ), 1)
    bias = jnp.where((kpos >= qpos - WINDOW) & (kpos <= qpos), 0.0, NEG)
    q = q_ref[...] * SCALE
    pieces, slabs = [], []
    for k in range(KV_C):
        rows, sinks = [], []
        for g in range(grp):
            h = k * grp + g
            rows.append(_pad_head(q[:, (h // 2) * LANES:(h // 2 + 1) * LANES], h, k))
            sinks.append(jnp.broadcast_to(sink_ref[h:h + 1, :], (TQ, LANES)))
        qk = jnp.concatenate(rows, axis=0).astype(BF16)
        sink = jnp.concatenate(sinks, axis=0)
        s = lax.dot_general(qk, k_slab, _NT, preferred_element_type=F32)
        s = (s.reshape(grp, TQ, 2 * TQ) + bias[None]).reshape(grp * TQ, 2 * TQ)
        s0, s1 = s[:, 0:LANES], s[:, LANES:2 * LANES]
        m = jnp.maximum(jnp.max(jnp.maximum(s0, s1), axis=1, keepdims=True), sink)
        p = jnp.concatenate([jnp.exp(s0 - m), jnp.exp(s1 - m)], axis=1).astype(BF16)
        acc = jnp.dot(p, _with_ones(v_slab, k), preferred_element_type=F32)
        acc = acc + jnp.where(_own_half(acc.shape, k), 0.0, jnp.exp(sink - m))
        for g in range(grp):
            pieces.append(acc[g * TQ:(g + 1) * TQ])
            slabs.append(k)
    _gather_heads_norm(o_ref, pieces, slabs)


def _attn_c(q, kv_bf, sinks, nb, s):
    nq = s // TQ
    assert nq >= 2 and TQ == WINDOW
    return pl.pallas_call(
        _attn_c_kernel, grid=(nb, nq),
        in_specs=[pl.BlockSpec((TQ, q.shape[1]), lambda b, i: (b * nq + i, 0)),
                  pl.BlockSpec((s, 2 * LANES), lambda b, i: (b, 0)),
                  pl.BlockSpec((H_C, 1), lambda b, i: (0, 0))],
        out_specs=pl.BlockSpec((TQ, H_C * HEAD_DIM), lambda b, i: (b * nq + i, 0)),
        out_shape=jax.ShapeDtypeStruct((nb * s, H_C * HEAD_DIM), F32),
        compiler_params=_cp("arbitrary", "arbitrary"), name="attn_c_prompt",
    )(q, kv_bf, sinks.reshape(H_C, 1))


def _page_specs(rows, layer, pps):
    def make(u):
        return pl.BlockSpec((None, None, rows, LANES), lambda b, g, pt, *_: (layer, pt[b, g * pps + u], 0, 0))
    return [make(u) for u in range(pps)]


def _sa1_kernel(pt_ref, q_ref, w_ref, *refs, pps):
    pages, o_ref = refs[:pps], refs[pps]
    q, w = q_ref[...], w_ref[...]
    for u in range(pps):
        s = jnp.dot(q, pages[u][...].astype(BF16), preferred_element_type=F32)
        o_ref[:, u * LANES:(u + 1) * LANES] = jnp.sum(jnp.maximum(s, 0.0) * w, axis=0, keepdims=True)


def _sa1(page_table, q_blk, w_blk, idx_t, layer, pps):
    nseq, npages = page_table.shape
    gs = pltpu.PrefetchScalarGridSpec(
        num_scalar_prefetch=1, grid=(nseq, npages // pps),
        in_specs=[pl.BlockSpec((None, SUBLANES, IDX_DIM), lambda b, g, pt: (b, 0, 0)),
                  pl.BlockSpec((None, SUBLANES, LANES), lambda b, g, pt: (b, 0, 0))] + _page_specs(IDX_DIM, layer, pps),
        out_specs=pl.BlockSpec((None, 1, pps * LANES), lambda b, g, pt: (b, 0, g)))
    return pl.pallas_call(
        functools.partial(_sa1_kernel, pps=pps), grid_spec=gs,
        out_shape=jax.ShapeDtypeStruct((nseq, 1, npages * PAGE_SIZE), F32),
        compiler_params=_cp("arbitrary", "arbitrary"), name="index_scores_sample",
    )(page_table, q_blk, w_blk, *([idx_t] * pps))


def _sa2_kernel(i_ref, qi_ref, ki_ref, wi_ref, tau_ref, cut_ref, new_ref, key_s, *, n_sel, idx_bits):
    nseq, p = i_ref.shape
    nt = p // LANES
    for t in range(nt):
        key_s[t] = _to_key(i_ref[:, t * LANES:(t + 1) * LANES])
    qi = qi_ref[...].astype(BF16).astype(F32)
    ki = ki_ref[...].astype(BF16).astype(F32)
    inew = jnp.zeros((nseq, 1), F32)
    for h in range(IDX_HEADS):
        sh = jnp.sum(qi[:, h * IDX_DIM:(h + 1) * IDX_DIM] * ki, axis=1, keepdims=True)
        inew = inew + jnp.maximum(sh, 0.0) * wi_ref[:, h * LANES:h * LANES + 1]
    lane = lax.broadcasted_iota(I32, (nseq, LANES), 1)
    key_new = _to_key(jnp.broadcast_to(inew, (nseq, LANES)))
    key_s[nt] = jnp.where(lane == 0, key_new, PAD_KEY)
    for t in range(nt + 1, key_s.shape[0]):
        key_s[t] = jnp.full((nseq, LANES), PAD_KEY, I32)
    kk = jnp.full((nseq, 1), n_sel, F32)
    tau, cut = _select_threshold(key_s, key_s.shape[0] // CH, kk, idx_bits)
    tau_ref[...] = tau
    cut_ref[...] = cut
    kn = key_new[:, 0:1]
    new_ref[...] = ((kn > tau) | ((kn == tau) & (p <= cut))).astype(I32)


def _sa2(scores, qi, ki, wi):
    nseq, p = scores.shape
    ntiles = -(-(p // LANES + 1) // CH) * CH
    n_sel = min(TOPK_MAX, (p + 1) // 4)
    return pl.pallas_call(
        functools.partial(_sa2_kernel, n_sel=n_sel, idx_bits=p.bit_length()),
        out_shape=[jax.ShapeDtypeStruct((nseq, 1), I32)] * 3,
        scratch_shapes=[pltpu.VMEM((ntiles, nseq, LANES), I32)],
        compiler_params=pltpu.CompilerParams(vmem_limit_bytes=VMEM_LIMIT), name="topk_threshold_sample",
    )(scores, qi, ki, wi)


def _sa3_kernel(pt_ref, tau_ref, cut_ref, new_ref, q_ref, i_ref, kvn_ref, *refs, pps, ng):
    pages, o_ref, m_s, l_s, acc_s = refs[:pps], refs[pps], refs[pps + 1], refs[pps + 2], refs[pps + 3]
    b, g = pl.program_id(0), pl.program_id(1)

    @pl.when(g == 0)
    def _():
        _init_softmax(m_s, l_s, acc_s)

    tau, cut = tau_ref[b], cut_ref[b]
    q = q_ref[...]
    lane = lax.broadcasted_iota(I32, (1, LANES), 1)
    ss = []
    for u in range(pps):
        key = _to_key(i_ref[:, u * LANES:(u + 1) * LANES])
        sel = (key > tau) | ((key == tau) & ((g * pps + u) * PAGE_SIZE + lane <= cut))
        s = jnp.dot(q, pages[u][0:LANES, :].astype(BF16), preferred_element_type=F32) * SCALE
        ss.append(s + jnp.where(sel, 0.0, NEG))
    s = jnp.concatenate(ss, axis=1)
    m_old = m_s[0]
    m_new = jnp.maximum(m_old, jnp.max(s, axis=1, keepdims=True))
    alpha = jnp.exp(m_old - m_new)
    p = jnp.exp(s - m_new)
    l_s[0] = alpha * l_s[0] + jnp.sum(p, axis=1, keepdims=True)
    pv = jnp.zeros(acc_s.shape[1:], F32)
    for u in range(pps):
        pv = pv + lax.dot_general(p[:, u * LANES:(u + 1) * LANES].astype(BF16),
                                  pages[u][LANES:2 * LANES, :].astype(BF16), _NT, preferred_element_type=F32)
    acc_s[0] = alpha * acc_s[0] + pv
    m_s[0] = m_new

    @pl.when(g == ng - 1)
    def _():
        kvn = kvn_ref[...].astype(BF16).astype(F32)
        s_new = jnp.sum(q.astype(F32) * kvn[:, 0:LANES], axis=1, keepdims=True) * SCALE
        s_new = jnp.where(new_ref[b] > 0, s_new, NEG)
        m_fin = jnp.maximum(m_s[0], s_new)
        a_fin = jnp.exp(m_s[0] - m_fin)
        p_new = jnp.exp(s_new - m_fin)
        den = a_fin * l_s[0] + p_new
        o_ref[...] = (a_fin * acc_s[0] + p_new * kvn[:, LANES:2 * LANES]) / den


def _sa3(page_table, tau, cut, new, q_blk, scores, kv_new, kv_t, layer, pps):
    nseq, npages = page_table.shape
    ng = npages // pps
    nh = q_blk.shape[1]
    gs = pltpu.PrefetchScalarGridSpec(
        num_scalar_prefetch=4, grid=(nseq, ng),
        in_specs=[pl.BlockSpec((None, nh, LANES), lambda b, g, *_: (b, 0, 0)),
                  pl.BlockSpec((None, 1, pps * LANES), lambda b, g, *_: (b, 0, g)),
                  pl.BlockSpec((None, 1, 2 * LANES), lambda b, g, *_: (b, 0, 0))] + _page_specs(2 * LANES, layer, pps),
        out_specs=pl.BlockSpec((None, nh, LANES), lambda b, g, *_: (b, 0, 0)),
        scratch_shapes=[pltpu.VMEM((1, nh, 1), F32), pltpu.VMEM((1, nh, 1), F32), pltpu.VMEM((1, nh, LANES), F32)])
    return pl.pallas_call(
        functools.partial(_sa3_kernel, pps=pps, ng=ng), grid_spec=gs,
        out_shape=jax.ShapeDtypeStruct((nseq, nh, LANES), F32),
        compiler_params=_cp("arbitrary", "arbitrary"), name="attn_a_sample",
    )(page_table, tau, cut, new, q_blk, scores, kv_new, *([kv_t] * pps))


def _sb_kernel(pt_ref, q_ref, kvn_ref, lam_ref, sg_ref, *refs, pps, ng, lam_init):
    pages, o_ref, m_s, l_s, acc_s = refs[:pps], refs[pps], refs[pps + 1], refs[pps + 2], refs[pps + 3]
    g = pl.program_id(1)
    grp = H_B // KV_B
    stride = 2 * KV_B

    @pl.when(g == 0)
    def _():
        _init_softmax(m_s, l_s, acc_s)

    for k in range(KV_B):
        q = q_ref[k]
        ss = []
        for u in range(pps):
            k_rows = pages[u][pl.ds(k, PAGE_SIZE, stride=stride), :].astype(BF16)
            ss.append(lax.dot_general(q, k_rows, _NT, preferred_element_type=F32) * SCALE)
        s = jnp.concatenate(ss, axis=1)
        m_old = m_s[k]
        m_new = jnp.maximum(m_old, jnp.max(s, axis=1, keepdims=True))
        alpha = jnp.exp(m_old - m_new)
        p = jnp.exp(s - m_new)
        l_s[k] = alpha * l_s[k] + jnp.sum(p, axis=1, keepdims=True)
        pv = jnp.zeros(acc_s.shape[1:], F32)
        for u in range(pps):
            v_rows = pages[u][pl.ds(KV_B + k, PAGE_SIZE, stride=stride), :].astype(BF16)
            pv = pv + jnp.dot(p[:, u * LANES:(u + 1) * LANES].astype(BF16), v_rows, preferred_element_type=F32)
        acc_s[k] = alpha * acc_s[k] + pv
        m_s[k] = m_new

    @pl.when(g == ng - 1)
    def _():
        lam = _lambda_value(lam_ref, lam_init)
        kvn = kvn_ref[...].astype(BF16).astype(F32)
        for k in range(KV_B):
            q = q_ref[k].astype(F32)
            s_new = jnp.sum(q * kvn[:, k * LANES:(k + 1) * LANES], axis=1, keepdims=True) * SCALE
            m_fin = jnp.maximum(m_s[k], s_new)
            a_fin = jnp.exp(m_s[k] - m_fin)
            p_new = jnp.exp(s_new - m_fin)
            den = a_fin * l_s[k] + p_new
            o = (a_fin * acc_s[k] + p_new * kvn[:, (KV_B + k) * LANES:(KV_B + k + 1) * LANES]) / den
            for gg in range(grp):
                d = o[2 * gg:2 * gg + 1, :] - lam * o[2 * gg + 1:2 * gg + 2, :]
                h = k * grp + gg
                o_ref[h:h + 1, :] = _rmsnorm_rows(d, sg_ref[...]) * (1.0 - lam_init)


def _sb(page_table, q_blk, kv_new, lam_p, subln_g, kv_v, layer, pps, lam_init):
    nseq, npages = page_table.shape
    ng = npages // pps
    rows = q_blk.shape[2]
    gs = pltpu.PrefetchScalarGridSpec(
        num_scalar_prefetch=1, grid=(nseq, ng),
        in_specs=[pl.BlockSpec((None, KV_B, rows, LANES), lambda b, g, pt: (b, 0, 0, 0)),
                  pl.BlockSpec((None, 1, 2 * KV_B * LANES), lambda b, g, pt: (b, 0, 0)),
                  pl.BlockSpec(lam_p.shape, lambda b, g, pt: (0, 0)),
                  pl.BlockSpec((1, 2 * HEAD_DIM), lambda b, g, pt: (0, 0))]
                 + _page_specs(PAGE_SIZE * 2 * KV_B, layer, pps),
        out_specs=pl.BlockSpec((None, H_B, LANES), lambda b, g, pt: (b, 0, 0)),
        scratch_shapes=[pltpu.VMEM((KV_B, rows, 1), F32), pltpu.VMEM((KV_B, rows, 1), F32),
                        pltpu.VMEM((KV_B, rows, LANES), F32)])
    return pl.pallas_call(
        functools.partial(_sb_kernel, pps=pps, ng=ng, lam_init=lam_init), grid_spec=gs,
        out_shape=jax.ShapeDtypeStruct((nseq, H_B, LANES), F32),
        compiler_params=_cp("arbitrary", "arbitrary"), name="attn_b_sample",
    )(page_table, q_blk, kv_new, lam_p, subln_g.reshape(1, 2 * HEAD_DIM), *([kv_v] * pps))


def _sc_kernel(q_ref, kvn_ref, sink_ref, page_ref, o_ref):
    q = q_ref[...]
    kvn = kvn_ref[...].astype(BF16).astype(F32)
    sink = sink_ref[...]
    s = jnp.dot(q, page_ref[0:LANES, :].astype(BF16), preferred_element_type=F32) * SCALE
    s_new = jnp.sum(q.astype(F32) * kvn[:, 0:LANES], axis=1, keepdims=True) * SCALE
    m = jnp.maximum(jnp.maximum(jnp.max(s, axis=1, keepdims=True), s_new), sink)
    p = jnp.exp(s - m)
    p_new = jnp.exp(s_new - m)
    den = jnp.sum(p, axis=1, keepdims=True) + p_new + jnp.exp(sink - m)
    pv = lax.dot_general(p.astype(BF16), page_ref[LANES:2 * LANES, :].astype(BF16), _NT, preferred_element_type=F32)
    o_ref[...] = (pv + p_new * kvn[:, LANES:2 * LANES]) / den


def _sc(q_blk, kv_new, sinks, buf_t, layer):
    nseq = q_blk.shape[0]
    return pl.pallas_call(
        _sc_kernel, grid=(nseq,),
        in_specs=[pl.BlockSpec((None, H_C, LANES), lambda b: (b, 0, 0)),
                  pl.BlockSpec((None, 1, 2 * LANES), lambda b: (b, 0, 0)),
                  pl.BlockSpec((H_C, 1), lambda b: (0, 0)),
                  pl.BlockSpec((None, None, 2 * LANES, buf_t.shape[3]), lambda b: (layer, b, 0, 0))],
        out_specs=pl.BlockSpec((None, H_C, LANES), lambda b: (b, 0, 0)),
        out_shape=jax.ShapeDtypeStruct((nseq, H_C, LANES), F32),
        compiler_params=_cp("arbitrary"), name="attn_c_sample",
    )(q_blk, kv_new, sinks.reshape(H_C, 1), buf_t)


def _slab_pad(q, n_heads, grp):
    n = q.shape[0]
    qh = q.reshape(n, n_heads // grp, grp, HEAD_DIM)
    z = jnp.zeros_like(qh)
    lo = jnp.concatenate([qh[:, 0:1], z[:, 0:1]], axis=-1)
    hi = jnp.concatenate([z[:, 1:2], qh[:, 1:2]], axis=-1)
    return jnp.concatenate([lo, hi], axis=1).reshape(n, n_heads, 2 * HEAD_DIM)


def _slab_take(o, grp):
    n, n_heads, _ = o.shape
    oh = o.reshape(n, n_heads // grp, grp, 2, HEAD_DIM)
    return jnp.concatenate([oh[:, 0:1, :, 0], oh[:, 1:2, :, 1]], axis=1).reshape(n, n_heads * HEAD_DIM)


def kernel(x_prompt, x_sample, cache_a_kv, cache_a_idx, cache_b_kv, state_c_kv, state_ffn_conv, page_table, norm_mix_g, norm_ffn_g, norm_final_g, w_in_even, w_out_even, b_lambda, b_subln_g, w_in_odd, w_out_odd, c_sinks, ffn_w_up, ffn_conv_w, ffn_conv_b, ffn_w_down):
    nb, s, d = x_prompt.shape
    nseq = x_sample.shape[0]
    npages = page_table.shape[1]
    past = npages * PAGE_SIZE
    ff = ffn_w_down.shape[1]
    wb = state_c_kv.shape[2]
    assert s % TK == 0 and x_sample.shape[1] == 1 and wb <= WINDOW and wb == LANES
    assert KV_A == 2 and KV_B == 2 and KV_C == 2 and nseq % SUBLANES == 0

    tm_p = min(512, s)
    tm_f = min(512, s)
    tf_p = _pick_tile(ff, 1408)
    tf_s = _pick_tile(ff, 1408)
    tq_a = min(256, s)
    pps = min(64, npages)
    assert npages % pps == 0

    xp = x_prompt.reshape(nb * s, d)
    xs = x_sample.reshape(nseq, d)
    cos_p, sin_p = _rope_tables(jnp.arange(s))
    cos_s, sin_s = _rope_tables(jnp.full((nseq,), past))

    idx_t = jnp.transpose(cache_a_idx, (0, 1, 3, 2))
    akv_t = jnp.transpose(cache_a_kv, (0, 1, 3, 4, 5, 2)).reshape(cache_a_kv.shape[0], cache_a_kv.shape[1], 2 * LANES, PAGE_SIZE)
    bkv_v = cache_b_kv.reshape(cache_b_kv.shape[0], cache_b_kv.shape[1], PAGE_SIZE * 2 * KV_B, 2 * HEAD_DIM)
    ckv_t = jnp.transpose(state_c_kv, (0, 1, 3, 4, 5, 2)).reshape(state_c_kv.shape[0], nseq, 2 * LANES, wb)

    akv_p, akv_s, aidx_p, aidx_s, bkv_p, bkv_s = [], [], [], [], [], []
    ckv_p, ckv_s, conv_p, conv_s = [], [], [], []
    for l in range(DEPTH):
        j = l // 2
        if l % 2 == 0:
            lam_init = 0.8 - 0.6 * math.exp(-0.3 * l)
            w_in = _even_weights(w_in_even[j])
            w_out = w_out_even[j].astype(BF16)
            wo = [w_out[:H_A * HEAD_DIM], w_out[H_A * HEAD_DIM:]]
            qa, kva, qi, ki, wi, qb, kvb, kva_bf, ki_bf, kvb_bf = _proj(
                xp, norm_mix_g[l], w_in, cos_p, sin_p, _EVEN_PLAN, _EVEN_WIDTHS, tm_p)
            oa = _attn_a(qa, qi, wi, ki_bf, kva_bf, nb, s, tq_a)
            ob = _attn_b(qb, kvb_bf, b_lambda[j], b_subln_g[j], lam_init, nb, s, tq_a)
            xp = _outproj([oa, ob], xp, wo, tm_p)
            akv_p.append(kva.reshape(nb, s, 2, KV_A, HEAD_DIM))
            aidx_p.append(ki.reshape(nb, s, IDX_DIM))
            bkv_p.append(kvb.reshape(nb, s, 2, KV_B, 2 * HEAD_DIM))
            qa, kva, qi, ki, wi, qb, kvb = _proj(xs, norm_mix_g[l], w_in, cos_s, sin_s, _EVEN_PLAN, _EVEN_WIDTHS, nseq)[:7]
            zrow = SUBLANES - IDX_HEADS
            qi_blk = jnp.pad(qi.reshape(nseq, IDX_HEADS, IDX_DIM), ((0, 0), (0, zrow), (0, 0))).astype(BF16)
            wi_blk = jnp.pad(wi.reshape(nseq, IDX_HEADS, LANES), ((0, 0), (0, zrow), (0, 0)))
            scores = _sa1(page_table, qi_blk, wi_blk, idx_t, j, pps)
            tau, cut, new = _sa2(scores.reshape(nseq, past), qi, ki, wi)
            qa_blk = _slab_pad(qa, H_A, H_A // KV_A).astype(BF16)
            oa = _sa3(page_table, tau.reshape(nseq), cut.reshape(nseq), new.reshape(nseq), qa_blk, scores,
                      kva.reshape(nseq, 1, 2 * LANES), akv_t, j, pps)
            oa = _slab_take(oa, H_A // KV_A)
            qh = qb.reshape(nseq, KV_B, H_B // KV_B, 2, HEAD_DIM)
            z = jnp.zeros_like(qh[..., 0, :])
            qb_blk = jnp.stack([jnp.concatenate([qh[..., 0, :], z], -1), jnp.concatenate([z, qh[..., 1, :]], -1)], axis=3)
            qb_blk = qb_blk.reshape(nseq, KV_B, 2 * (H_B // KV_B), 2 * HEAD_DIM)
            qb_blk = jnp.pad(qb_blk, ((0, 0), (0, 0), (0, SUBLANES - qb_blk.shape[2]), (0, 0))).astype(BF16)
            ob = _sb(page_table, qb_blk, kvb.reshape(nseq, 1, 2 * KV_B * LANES), b_lambda[j], b_subln_g[j], bkv_v, j, pps, lam_init)
            xs = _outproj([oa, ob.reshape(nseq, H_B * 2 * HEAD_DIM)], xs, wo, nseq)
            akv_s.append(kva.reshape(nseq, 1, 2, KV_A, HEAD_DIM))
            aidx_s.append(ki.reshape(nseq, 1, IDX_DIM))
            bkv_s.append(kvb.reshape(nseq, 1, 2, KV_B, 2 * HEAD_DIM))
        else:
            w_in = w_in_odd[j].astype(BF16)
            w_out = w_out_odd[j].astype(BF16)
            q, kv, kv_bf = _proj(xp, norm_mix_g[l], w_in, cos_p, sin_p, _ODD_PLAN, _ODD_WIDTHS, tm_p)
            o = _attn_c(q, kv_bf, c_sinks[j], nb, s)
            xp = _outproj([o], xp, [w_out], tm_p)
            ckv_p.append(kv.reshape(nb, s, 2, KV_C, HEAD_DIM)[:, s - min(WINDOW, s):])
            q, kv = _proj(xs, norm_mix_g[l], w_in, cos_s, sin_s, _ODD_PLAN, _ODD_WIDTHS, nseq)[:2]
            q_blk = _slab_pad(q, H_C, H_C // KV_C).astype(BF16)
            o = _sc(q_blk, kv.reshape(nseq, 1, 2 * LANES), c_sinks[j], ckv_t, j)
            xs = _outproj([_slab_take(o, H_C // KV_C)], xs, [w_out], nseq)
            ckv_s.append(jnp.concatenate([state_c_kv[j][:, 1:], kv.reshape(nseq, 1, 2, KV_C, HEAD_DIM)], axis=1))
        w_up = ffn_w_up[l].astype(BF16)
        wg, wv, wd = w_up[:, :ff], w_up[:, ff:], ffn_w_down[l].astype(BF16)
        xp, tail = _ffn(xp, norm_ffn_g[l], wg, wv, ffn_conv_w[l], ffn_conv_b[l], wd, tm_f, tf_p, rows_per_seq=s)
        conv_p.append(tail.reshape(nb, s // tm_f, SUBLANES, ff)[:, -1, SUBLANES - (CONV_W - 1):, :])
        hist = state_ffn_conv[l]
        xs, g_new = _ffn(xs, norm_ffn_g[l], wg, wv, ffn_conv_w[l], ffn_conv_b[l], wd, nseq, tf_s, hist=hist)
        conv_s.append(jnp.concatenate([hist[:, 1:], g_new[:, None, :]], axis=1))
    y_prompt = _final_norm(xp, norm_final_g, tm_p).reshape(nb, s, d)
    y_sample = _final_norm(xs, norm_final_g, nseq).reshape(nseq, 1, d)
    return (y_prompt, y_sample,
            jnp.stack(akv_p), jnp.stack(akv_s), jnp.stack(aidx_p), jnp.stack(aidx_s),
            jnp.stack(bkv_p), jnp.stack(bkv_s), jnp.stack(ckv_p), jnp.stack(ckv_s),
            jnp.stack(conv_p), jnp.stack(conv_s))
```
